```python
import numpy as np
import jax
import jax.numpy as jnp
from jax import lax

D_MODEL = 1024
BATCH = 8
SEQ = 8192
DEPTH = 1

GLA_HEADS = 4
GLA_DK = D_MODEL // 2
GLA_DV = D_MODEL
GLA_HEAD_DK = GLA_DK // GLA_HEADS
GLA_HEAD_DV = GLA_DV // GLA_HEADS
GLA_RANK = 16
GLA_TAU = 16.0
GLA_CHUNK = 64

LRU_WIDTH = D_MODEL
LRU_BLOCKS = 8
LRU_BLOCK_W = LRU_WIDTH // LRU_BLOCKS
CONV_WIDTH = 4
LRU_C = 8.0

PEER_HEADS = 8
PEER_NKEYS = 128
PEER_EXPERTS = PEER_NKEYS * PEER_NKEYS
PEER_DKEY = 256
PEER_HALF = PEER_DKEY // 2
PEER_TOPK = 16
PEER_TOKEN_BLOCK = 128

EPS = 1e-6

SPLIT_SIZES = (GLA_DK, GLA_DK, GLA_DV, GLA_DV, GLA_RANK, LRU_WIDTH, LRU_WIDTH, D_MODEL, D_MODEL)
D_IN = 2 * GLA_DK + 2 * GLA_DV + GLA_RANK + 2 * LRU_WIDTH + 2 * D_MODEL

kernel_name = 'hybrid_gla_rglru_peer_block'


def rms_norm(x, gain):
    xf = x.astype(jnp.float32)
    y = xf * lax.rsqrt(jnp.mean(xf * xf, axis=-1, keepdims=True) + EPS)
    return y * gain.astype(jnp.float32)


def gla_branch(q, k, v, r, a_low, w_alpha, b_alpha, gla_norm):
    f32 = jnp.float32
    B, S, _ = q.shape
    n_chunks = S // GLA_CHUNK
    log_alpha = jax.nn.log_sigmoid((a_low @ w_alpha + b_alpha).astype(f32)) / GLA_TAU

    def to_chunks(t, d):
        return t.astype(f32).reshape(B, n_chunks, GLA_CHUNK, GLA_HEADS, d).transpose(1, 0, 3, 2, 4)

    qc = to_chunks(q, GLA_HEAD_DK) * (GLA_HEAD_DK ** -0.5)
    kc = to_chunks(k, GLA_HEAD_DK)
    vc = to_chunks(v, GLA_HEAD_DV)
    gc = to_chunks(log_alpha, GLA_HEAD_DK)
    causal = jnp.tril(jnp.ones((GLA_CHUNK, GLA_CHUNK), dtype=bool))[:, :, None]

    def chunk_step(state, inp):
        qb, kb, vb, gb = inp
        cum = jnp.cumsum(gb, axis=2)
        cum_last = cum[:, :, -1:, :]
        o_inter = jnp.einsum('bhtk,bhkv->bhtv', qb * jnp.exp(cum), state)
        rel = cum[:, :, :, None, :] - cum[:, :, None, :, :]
        decay = jnp.exp(jnp.where(causal, rel, -jnp.inf))
        scores = jnp.einsum('bhtk,bhsk,bhtsk->bhts', qb, kb, decay)
        o_intra = jnp.einsum('bhts,bhsv->bhtv', scores, vb)
        k_dec = kb * jnp.exp(cum_last - cum)
        state = jnp.exp(cum_last[:, :, 0, :])[..., None] * state + jnp.einsum('bhsk,bhsv->bhkv', k_dec, vb)
        return state, o_inter + o_intra

    state0 = jnp.zeros((B, GLA_HEADS, GLA_HEAD_DK, GLA_HEAD_DV), f32)
    _, o = lax.scan(chunk_step, state0, (qc, kc, vc, gc))
    o = o.transpose(1, 0, 3, 2, 4).reshape(B, S, GLA_HEADS, GLA_HEAD_DV)
    o = rms_norm(o, gla_norm.reshape(GLA_HEADS, GLA_HEAD_DV)).reshape(B, S, GLA_DV)
    return o * jax.nn.silu(r.astype(f32))


def rg_lru_branch(xl, xg, conv_w, conv_b, lru_w_a, lru_b_a, lru_w_x, lru_b_x, lru_lambda):
    f32 = jnp.float32
    B, S, W = xl.shape
    xc = lax.conv_general_dilated(
        xl.astype(f32), conv_w.astype(f32)[:, None, :], window_strides=(1,),
        padding=[(CONV_WIDTH - 1, 0)], dimension_numbers=('NWC', 'WIO', 'NWC'),
        feature_group_count=W) + conv_b.astype(f32)
    xb = xc.reshape(B, S, LRU_BLOCKS, LRU_BLOCK_W)
    gate_r = jax.nn.sigmoid(jnp.einsum('bsni,nij->bsnj', xb, lru_w_a.astype(f32)) + lru_b_a.astype(f32)).reshape(B, S, W)
    gate_i = jax.nn.sigmoid(jnp.einsum('bsni,nij->bsnj', xb, lru_w_x.astype(f32)) + lru_b_x.astype(f32)).reshape(B, S, W)
    log_a = -LRU_C * gate_r * jax.nn.softplus(-lru_lambda.astype(f32))
    a = jnp.exp(log_a)
    u = jnp.sqrt(jnp.maximum(-jnp.expm1(2.0 * log_a), 0.0)) * (gate_i * xc)

    def combine(left, right):
        a_l, h_l = left
        a_r, h_r = right
        return a_l * a_r, a_r * h_l + h_r

    _, h = lax.associative_scan(combine, (a, u), axis=1)
    return h * jax.nn.gelu(xg.astype(f32))


def token_mixers(h, w_in, w_alpha, b_alpha, gla_norm, conv_w, conv_b, lru_w_a, lru_b_a,
                 lru_w_x, lru_b_x, lru_lambda, w_out):
    proj = h @ w_in
    offsets = np.cumsum(SPLIT_SIZES)[:-1].tolist()
    q, k, v, r, a_low, xl, xg, g_a, g_b = jnp.split(proj, offsets, axis=-1)
    y_gla = gla_branch(q, k, v, r, a_low, w_alpha, b_alpha, gla_norm)
    y_lru = rg_lru_branch(xl, xg, conv_w, conv_b, lru_w_a, lru_b_a, lru_w_x, lru_b_x, lru_lambda)
    merged = (jax.nn.sigmoid(g_a.astype(jnp.float32)) * y_gla
              + jax.nn.sigmoid(g_b.astype(jnp.float32)) * y_lru)
    return merged.astype(h.dtype) @ w_out


def peer_tokens(xt, w_query, sub_keys, u_table, v_table):
    f32 = jnp.float32
    T = xt.shape[0]
    qry = (xt @ w_query).astype(f32).reshape(T, PEER_HEADS, 2, PEER_HALF)
    sub_scores = jnp.einsum('thpd,hpnd->thpn', qry, sub_keys.astype(f32))
    top_s, top_i = lax.top_k(sub_scores, PEER_TOPK)
    cand_s = (top_s[:, :, 0, :, None] + top_s[:, :, 1, None, :]).reshape(T, PEER_HEADS, PEER_TOPK * PEER_TOPK)
    cand_i = (top_i[:, :, 0, :, None] * PEER_NKEYS + top_i[:, :, 1, None, :]).reshape(T, PEER_HEADS, PEER_TOPK * PEER_TOPK)
    best_s, best_pos = lax.top_k(cand_s, PEER_TOPK)
    expert_idx = jnp.take_along_axis(cand_i, best_pos, axis=-1)
    gates = jax.nn.softmax(best_s, axis=-1)
    u_sel = jnp.take(u_table, expert_idx, axis=0)
    act = jax.nn.gelu(jnp.einsum('td,thkd->thk', xt, u_sel).astype(f32))
    v_sel = jnp.take(v_table, expert_idx, axis=0)
    return jnp.einsum('thk,thkd->td', (gates * act).astype(v_sel.dtype), v_sel)


def peer_ffn(h, w_query, sub_keys, u_table, v_table):
    B, S, D = h.shape
    blocks = h.reshape(-1, PEER_TOKEN_BLOCK, D)
    out = lax.map(lambda xt: peer_tokens(xt, w_query, sub_keys, u_table, v_table), blocks)
    return out.reshape(B, S, D)


def hybrid_layer(x, c, w_ada, b_ada, norm_pre_mix, norm_post_mix, w_in, w_alpha, b_alpha, gla_norm,
                 conv_w, conv_b, lru_w_a, lru_b_a, lru_w_x, lru_b_x, lru_lambda, w_out,
                 norm_pre_ffn, norm_post_ffn, peer_w_query, peer_sub_keys, peer_u, peer_v):
    f32 = jnp.float32
    mod = jax.nn.silu(c.astype(f32)) @ w_ada.astype(f32) + b_ada.astype(f32)
    shift1, scale1, gate1, shift2, scale2, gate2 = jnp.split(mod[:, None, :], 6, axis=-1)
    h = rms_norm(x, norm_pre_mix) * (1.0 + scale1) + shift1
    y = token_mixers(h, w_in, w_alpha, b_alpha, gla_norm, conv_w, conv_b, lru_w_a, lru_b_a,
                     lru_w_x, lru_b_x, lru_lambda, w_out)
    x = x + (gate1 * rms_norm(y, norm_post_mix)).astype(x.dtype)
    h = rms_norm(x, norm_pre_ffn) * (1.0 + scale2) + shift2
    y = peer_ffn(h, peer_w_query, peer_sub_keys, peer_u, peer_v)
    x = x + (gate2 * rms_norm(y, norm_post_ffn)).astype(x.dtype)
    return x


def setup_inputs(seed: int = 0) -> dict:
    key = jax.random.key(seed)
    ks = jax.random.split(key, 26)
    D = D_MODEL
    nrm = jax.random.normal

    def gain(k, n):
        return 1.0 + 0.05 * nrm(k, (DEPTH, n), jnp.float32)

    lam_u = jax.random.uniform(ks[17], (DEPTH, LRU_WIDTH), jnp.float32, minval=0.9, maxval=0.999)
    lam_p = lam_u ** (1.0 / LRU_C)
    return {
        'x': nrm(ks[0], (BATCH, SEQ, D), jnp.float32),
        'c': nrm(ks[1], (BATCH, D), jnp.float32),
        'w_ada': 0.5 * D ** -0.5 * nrm(ks[2], (DEPTH, D, 6 * D), jnp.float32),
        'b_ada': 0.02 * nrm(ks[3], (DEPTH, 6 * D), jnp.float32),
        'norm_pre_mix': gain(ks[4], D),
        'norm_post_mix': gain(ks[5], D),
        'w_in': D ** -0.5 * nrm(ks[6], (DEPTH, D, D_IN), jnp.float32),
        'w_alpha': GLA_RANK ** -0.5 * nrm(ks[7], (DEPTH, GLA_RANK, GLA_DK), jnp.float32),
        'b_alpha': 0.1 * nrm(ks[8], (DEPTH, GLA_DK), jnp.float32),
        'gla_norm': gain(ks[9], GLA_DV),
        'conv_w': CONV_WIDTH ** -0.5 * nrm(ks[10], (DEPTH, CONV_WIDTH, LRU_WIDTH), jnp.float32),
        'conv_b': 0.02 * nrm(ks[11], (DEPTH, LRU_WIDTH), jnp.float32),
        'lru_w_a': LRU_BLOCK_W ** -0.5 * nrm(ks[12], (DEPTH, LRU_BLOCKS, LRU_BLOCK_W, LRU_BLOCK_W), jnp.float32),
        'lru_b_a': 0.02 * nrm(ks[13], (DEPTH, LRU_BLOCKS, LRU_BLOCK_W), jnp.float32),
        'lru_w_x': LRU_BLOCK_W ** -0.5 * nrm(ks[14], (DEPTH, LRU_BLOCKS, LRU_BLOCK_W, LRU_BLOCK_W), jnp.float32),
        'lru_b_x': 0.02 * nrm(ks[15], (DEPTH, LRU_BLOCKS, LRU_BLOCK_W), jnp.float32),
        'lru_lambda': jnp.log(lam_p) - jnp.log1p(-lam_p),
        'w_out': D ** -0.5 * nrm(ks[16], (DEPTH, D, D), jnp.float32),
        'norm_pre_ffn': gain(ks[18], D),
        'norm_post_ffn': gain(ks[19], D),
        'peer_w_query': D ** -0.5 * nrm(ks[20], (DEPTH, D, PEER_HEADS * PEER_DKEY), jnp.float32),
        'peer_sub_keys': PEER_HALF ** -0.5 * nrm(ks[21], (DEPTH, PEER_HEADS, 2, PEER_NKEYS, PEER_HALF), jnp.float32),
        'peer_u': D ** -0.5 * nrm(ks[22], (DEPTH, PEER_EXPERTS, D), jnp.float32),
        'peer_v': D ** -0.5 * nrm(ks[23], (DEPTH, PEER_EXPERTS, D), jnp.float32),
    }


def reference(x, c, w_ada, b_ada, norm_pre_mix, norm_post_mix, w_in, w_alpha, b_alpha, gla_norm,
              conv_w, conv_b, lru_w_a, lru_b_a, lru_w_x, lru_b_x, lru_lambda, w_out,
              norm_pre_ffn, norm_post_ffn, peer_w_query, peer_sub_keys, peer_u, peer_v):
    for l in range(DEPTH):
        x = hybrid_layer(x, c, w_ada[l], b_ada[l], norm_pre_mix[l], norm_post_mix[l], w_in[l],
                         w_alpha[l], b_alpha[l], gla_norm[l], conv_w[l], conv_b[l], lru_w_a[l],
                         lru_b_a[l], lru_w_x[l], lru_b_x[l], lru_lambda[l], w_out[l],
                         norm_pre_ffn[l], norm_post_ffn[l], peer_w_query[l], peer_sub_keys[l],
                         peer_u[l], peer_v[l])
    return x
```

```python
import functools
import math

import jax
import jax.numpy as jnp
from jax import lax
from jax.experimental import pallas as pl
from jax.experimental.pallas import tpu as pltpu

F32 = jnp.float32
BF16 = jnp.bfloat16

EPS = 1e-6
GLA_HEADS = 4
GLA_RANK_PAD = 128
GLA_TAU = 16.0
GLA_CHUNK = 64
LRU_BLOCKS = 8
CONV_WIDTH = 4
LRU_C = 8.0
PEER_HEADS = 8
PEER_NKEYS = 128
PEER_TOPK = 16
SUBLANES = 8

VMEM_LIMIT = 56 * 1024 * 1024

NT_DIMS = (((1,), (1,)), ((), ()))
TN_DIMS = (((0,), (0,)), ((), ()))


def _dot(a, b):
    return jnp.dot(a, b, preferred_element_type=F32)


def _dot_nt(a, b):
    return lax.dot_general(a, b, NT_DIMS, preferred_element_type=F32)


def _dot_tn(a, b):
    return lax.dot_general(a, b, TN_DIMS, preferred_element_type=F32)


def _sigmoid(x):
    return 1.0 / (1.0 + jnp.exp(-x))


def _gelu_tanh(x):
    c0 = math.sqrt(2.0 / math.pi)
    return 0.5 * x * (1.0 + jnp.tanh(c0 * (x + 0.044715 * (x * x * x))))


def _rms(x, gain):
    ms = jnp.mean(x * x, axis=-1, keepdims=True)
    return x * lax.rsqrt(ms + EPS) * gain


def _params(*sem):
    return pltpu.CompilerParams(dimension_semantics=sem, vmem_limit_bytes=VMEM_LIMIT)


def _ada_kernel(c_ref, w_ref, b_ref, o_ref):
    c = c_ref[...]
    s = c * _sigmoid(c)
    o_ref[...] = jnp.dot(s, w_ref[...], precision=lax.Precision.HIGHEST,
                         preferred_element_type=F32) + b_ref[...]


def _ada(c, w_ada, b_ada):
    B, D = c.shape
    n_out = w_ada.shape[1]
    blk = 1024
    return pl.pallas_call(
        _ada_kernel,
        grid=(n_out // blk,),
        in_specs=[pl.BlockSpec((B, D), lambda j: (0, 0)),
                  pl.BlockSpec((D, blk), lambda j: (0, j)),
                  pl.BlockSpec((1, blk), lambda j: (0, j))],
        out_specs=pl.BlockSpec((B, blk), lambda j: (0, j)),
        out_shape=jax.ShapeDtypeStruct((B, n_out), F32),
        compiler_params=_params("arbitrary"),
        name="ada",
    )(c, w_ada, b_ada.reshape(1, n_out))


def _inproj_kernel(x_ref, mod_ref, gain_ref, w_ref, wal_ref, walpha_ref, balpha_ref,
                   qk_ref, v_ref, r_ref, xl_ref, xg_ref, ga_ref, gb_ref, la_ref):
    D = x_ref.shape[-1]
    x = x_ref[...]
    shift = mod_ref[0, 0:1, :]
    scale = mod_ref[0, 1:2, :]
    h = (_rms(x, gain_ref[...]) * (1.0 + scale) + shift).astype(BF16)
    for j, o_ref in enumerate((qk_ref, v_ref, r_ref, xl_ref, xg_ref, ga_ref, gb_ref)):
        o_ref[...] = _dot(h, w_ref[:, j * D:(j + 1) * D]).astype(BF16)
    a_low = _dot(h, wal_ref[...]).astype(BF16)
    z = _dot(a_low, walpha_ref[...]) + balpha_ref[...]
    log_sig = jnp.minimum(z, 0.0) - jnp.log(1.0 + jnp.exp(-jnp.abs(z)))
    la_ref[...] = log_sig * (1.0 / GLA_TAU)


def _inproj(x2, mod3, gain, w_main, w_al, w_alpha, b_alpha, B, S, tb):
    N, D = x2.shape
    dk = w_alpha.shape[1]
    nblk = S // tb
    tok = lambda b, s: (b * nblk + s, 0)
    const = lambda b, s: (0, 0)
    big = jax.ShapeDtypeStruct((N, D), BF16)
    return pl.pallas_call(
        _inproj_kernel,
        grid=(B, nblk),
        in_specs=[pl.BlockSpec((tb, D), tok),
                  pl.BlockSpec((1, 6, D), lambda b, s: (b, 0, 0)),
                  pl.BlockSpec((1, D), const),
                  pl.BlockSpec(w_main.shape, const, pipeline_mode=pl.Buffered(1)),
                  pl.BlockSpec(w_al.shape, const),
                  pl.BlockSpec(w_alpha.shape, const),
                  pl.BlockSpec((1, dk), const)],
        out_specs=[pl.BlockSpec((tb, D), tok)] * 7 + [pl.BlockSpec((tb, dk), tok)],
        out_shape=[big] * 7 + [jax.ShapeDtypeStruct((N, dk), F32)],
        compiler_params=_params("arbitrary", "arbitrary"),
        name="inproj",
    )(x2, mod3, gain, w_main, w_al, w_alpha, b_alpha)


def _gla_kernel(qk_ref, v_ref, r_ref, la_ref, gn_ref, tri_ref, o_ref, st_ref):
    tc = qk_ref.shape[0]
    dk_all = la_ref.shape[-1]
    hdk = dk_all // GLA_HEADS
    hdv = v_ref.shape[-1] // GLA_HEADS
    C = GLA_CHUNK
    mid = C // 2 - 1

    @pl.when(pl.program_id(1) == 0)
    def _():
        st_ref[...] = jnp.zeros_like(st_ref)

    la = la_ref[...]
    p0 = la.astype(BF16)
    rem = la - p0.astype(F32)
    p1 = rem.astype(BF16)
    p2 = (rem - p1.astype(F32)).astype(BF16)
    tri = tri_ref[...]
    cum = _dot(tri, p0) + _dot(tri, p1) + _dot(tri, p2)

    row = lax.broadcasted_iota(jnp.int32, (C, C), 0)
    col = lax.broadcasted_iota(jnp.int32, (C, C), 1)
    causal = row >= col
    q_scale = hdk ** -0.5

    for c in range(tc // C):
        rows = slice(c * C, (c + 1) * C)
        cumc = cum[rows]
        cm = cumc[mid:mid + 1]
        cl = cumc[C - 1:C]
        q = qk_ref[rows, 0:dk_all].astype(F32) * q_scale
        k = qk_ref[rows, dk_all:2 * dk_all].astype(F32)
        qt = q * jnp.exp(cumc - cm)
        kt = k * jnp.exp(cm - cumc)
        qi = (qt * jnp.exp(cm)).astype(BF16)
        kd = (kt * jnp.exp(cl - cm)).astype(BF16)
        e_last = jnp.exp(cl)
        qt = qt.astype(BF16)
        kt = kt.astype(BF16)
        for h in range(GLA_HEADS):
            ks = slice(h * hdk, (h + 1) * hdk)
            vs = slice(h * hdv, (h + 1) * hdv)
            scores = _dot_nt(qt[:, ks], kt[:, ks])
            scores = jnp.where(causal, scores, 0.0).astype(BF16)
            vh = v_ref[rows, vs]
            st = st_ref[h]
            o = _dot(scores, vh) + _dot_nt(qi[:, ks], st.astype(BF16))
            st_ref[h] = st * e_last[:, ks] + _dot_tn(vh, kd[:, ks])
            y = _rms(o, gn_ref[:, vs])
            rr = r_ref[rows, vs].astype(F32)
            o_ref[rows, vs] = (y * (rr * _sigmoid(rr))).astype(BF16)


def _gla(qk, v, r, la, gla_norm, B, S, tc):
    N, D = v.shape
    dk = la.shape[1]
    nblk = S // tc
    tok = lambda b, s: (b * nblk + s, 0)
    const = lambda b, s: (0, 0)
    idx = jnp.arange(tc)
    tri = ((idx[:, None] >= idx[None, :]) &
           (idx[:, None] // GLA_CHUNK == idx[None, :] // GLA_CHUNK)).astype(BF16)
    return pl.pallas_call(
        _gla_kernel,
        grid=(B, nblk),
        in_specs=[pl.BlockSpec((tc, 2 * dk), tok),
                  pl.BlockSpec((tc, D), tok),
                  pl.BlockSpec((tc, D), tok),
                  pl.BlockSpec((tc, dk), tok),
                  pl.BlockSpec((1, D), const),
                  pl.BlockSpec((tc, tc), const)],
        out_specs=pl.BlockSpec((tc, D), tok),
        out_shape=jax.ShapeDtypeStruct((N, D), BF16),
        scratch_shapes=[pltpu.VMEM((GLA_HEADS, D // GLA_HEADS, dk // GLA_HEADS), F32)],
        compiler_params=_params("arbitrary", "arbitrary"),
        name="gla",
    )(qk, v, r, la, gla_norm, tri)


def _lru_kernel(xl_ref, xg_ref, cw_ref, cb_ref, wa_ref, ba_ref, wx_ref, bx_ref, lam_ref,
                o_ref, xbuf_ref, h_ref):
    td, W = xl_ref.shape
    bw = W // LRU_BLOCKS
    pad = SUBLANES

    @pl.when(pl.program_id(1) == 0)
    def _():
        xbuf_ref[0:pad, :] = jnp.zeros((pad, W), F32)
        h_ref[...] = jnp.zeros_like(h_ref)

    xbuf_ref[pad:pad + td, :] = xl_ref[...].astype(F32)
    xc = cb_ref[...]
    for j in range(CONV_WIDTH):
        off = pad - (CONV_WIDTH - 1) + j
        xc = xc + cw_ref[j:j + 1, :] * xbuf_ref[off:off + td, :]
    xbuf_ref[0:pad, :] = xbuf_ref[td:td + pad, :]

    xcb = xc.astype(BF16)
    gr_parts, gi_parts = [], []
    for n in range(LRU_BLOCKS):
        blk = xcb[:, n * bw:(n + 1) * bw]
        gr_parts.append(_dot(blk, wa_ref[n]))
        gi_parts.append(_dot(blk, wx_ref[n]))
    gate_r = _sigmoid(jnp.concatenate(gr_parts, axis=1) + ba_ref[...])
    gate_i = _sigmoid(jnp.concatenate(gi_parts, axis=1) + bx_ref[...])
    neg_lam = -lam_ref[...]
    softplus = jnp.maximum(neg_lam, 0.0) + jnp.log(1.0 + jnp.exp(-jnp.abs(neg_lam)))
    log_a = (-LRU_C) * gate_r * softplus
    a = jnp.exp(log_a)
    u = jnp.sqrt(jnp.maximum(1.0 - jnp.exp(2.0 * log_a), 0.0)) * (gate_i * xc)

    row = lax.broadcasted_iota(jnp.int32, (td, W), 0)
    d = 1
    while d < td:
        keep = row >= d
        a_sh = pltpu.roll(a, d, 0)
        u_sh = pltpu.roll(u, d, 0)
        u = jnp.where(keep, a * u_sh + u, u)
        a = jnp.where(keep, a * a_sh, a)
        d *= 2
    h = u + a * h_ref[0:1, :]
    h_ref[...] = jnp.broadcast_to(h[td - 1:td, :], h_ref.shape)
    o_ref[...] = (h * _gelu_tanh(xg_ref[...].astype(F32))).astype(BF16)


def _lru(xl, xg, conv_w, conv_b, wa, ba, wx, bx, lam, B, S, td):
    N, W = xl.shape
    nblk = S // td
    tok = lambda b, s: (b * nblk + s, 0)
    const2 = lambda b, s: (0, 0)
    const3 = lambda b, s: (0, 0, 0)
    return pl.pallas_call(
        _lru_kernel,
        grid=(B, nblk),
        in_specs=[pl.BlockSpec((td, W), tok),
                  pl.BlockSpec((td, W), tok),
                  pl.BlockSpec(conv_w.shape, const2),
                  pl.BlockSpec((1, W), const2),
                  pl.BlockSpec(wa.shape, const3),
                  pl.BlockSpec((1, W), const2),
                  pl.BlockSpec(wx.shape, const3),
                  pl.BlockSpec((1, W), const2),
                  pl.BlockSpec((1, W), const2)],
        out_specs=pl.BlockSpec((td, W), tok),
        out_shape=jax.ShapeDtypeStruct((N, W), BF16),
        scratch_shapes=[pltpu.VMEM((td + SUBLANES, W), F32),
                        pltpu.VMEM((SUBLANES, W), F32)],
        compiler_params=_params("arbitrary", "arbitrary"),
        name="lru",
    )(xl, xg, conv_w, conv_b, wa, ba, wx, bx, lam)


def _mix_kernel(yg_ref, yl_ref, ga_ref, gb_ref, x_ref, mod_ref, wout_ref, gpost_ref,
                gpre_ref, wq_ref, keys_ref, x1_ref, h2_ref, ss_ref):
    merged = (_sigmoid(ga_ref[...].astype(F32)) * yg_ref[...].astype(F32)
              + _sigmoid(gb_ref[...].astype(F32)) * yl_ref[...].astype(F32))
    y = _dot(merged.astype(BF16), wout_ref[...])
    gate1 = mod_ref[0, 2:3, :]
    x1 = x_ref[...] + gate1 * _rms(y, gpost_ref[...])
    x1_ref[...] = x1
    shift2 = mod_ref[0, 3:4, :]
    scale2 = mod_ref[0, 4:5, :]
    h2 = (_rms(x1, gpre_ref[...]) * (1.0 + scale2) + shift2).astype(BF16)
    h2_ref[...] = h2
    qry = _dot(h2, wq_ref[...]).astype(BF16)
    half = keys_ref.shape[-1]
    for hp in range(keys_ref.shape[0]):
        ss_ref[hp] = _dot_nt(keys_ref[hp], qry[:, hp * half:(hp + 1) * half])


def _mix(yg, yl, ga, gb, x2, mod3, w_out, g_post, g_pre, w_q, keys, B, S, te):
    N, D = x2.shape
    nblk = S // te
    tok = lambda b, s: (b * nblk + s, 0)
    const2 = lambda b, s: (0, 0)
    nhp, nkeys, _ = keys.shape
    return pl.pallas_call(
        _mix_kernel,
        grid=(B, nblk),
        in_specs=[pl.BlockSpec((te, D), tok)] * 5 + [
            pl.BlockSpec((1, 6, D), lambda b, s: (b, 0, 0)),
            pl.BlockSpec(w_out.shape, const2),
            pl.BlockSpec((1, D), const2),
            pl.BlockSpec((1, D), const2),
            pl.BlockSpec(w_q.shape, const2),
            pl.BlockSpec(keys.shape, lambda b, s: (0, 0, 0))],
        out_specs=[pl.BlockSpec((te, D), tok),
                   pl.BlockSpec((te, D), tok),
                   pl.BlockSpec((nhp, nkeys, te), lambda b, s: (0, 0, b * nblk + s))],
        out_shape=[jax.ShapeDtypeStruct((N, D), F32),
                   jax.ShapeDtypeStruct((N, D), BF16),
                   jax.ShapeDtypeStruct((nhp, nkeys, N), F32)],
        compiler_params=_params("arbitrary", "arbitrary"),
        name="mix",
    )(yg, yl, ga, gb, x2, mod3, w_out, g_post, g_pre, w_q, keys)


def _oddeven_merge_sort_pairs(n):
    pairs = []
    p = 1
    while p < n:
        k = p
        while k >= 1:
            for j in range(k % p, n - k, 2 * k):
                for i in range(min(k, n - j - k)):
                    if (i + j) // (2 * p) == (i + j + k) // (2 * p):
                        pairs.append((i + j, i + j + k))
            k //= 2
        p *= 2
    return pairs


_SORT16 = _oddeven_merge_sort_pairs(PEER_TOPK)


def _sort_desc(vals):
    vals = list(vals)
    for i, j in _SORT16:
        hi = jnp.maximum(vals[i], vals[j])
        lo = jnp.minimum(vals[i], vals[j])
        vals[i], vals[j] = hi, lo
    return vals


def _bitonic_merge_desc(vals):
    vals = list(vals)
    n = len(vals)
    d = n // 2
    while d >= 1:
        for i in range(n):
            if (i & d) == 0:
                hi = jnp.maximum(vals[i], vals[i + d])
                lo = jnp.minimum(vals[i], vals[i + d])
                vals[i], vals[i + d] = hi, lo
        d //= 2
    return vals


def _merge_across_sublanes(vals):
    n = len(vals)
    for shift in (4, 2, 1):
        partner = [pltpu.roll(v, shift, 0) for v in vals]
        vals = _bitonic_merge_desc([jnp.maximum(vals[i], partner[n - 1 - i]) for i in range(n)])
    return vals


def _top16_sorted(s_ref, hp):
    groups = [s_ref[hp, SUBLANES * i:SUBLANES * (i + 1), :] for i in range(PEER_NKEYS // SUBLANES)]
    return _merge_across_sublanes(_sort_desc(groups))


def _route_kernel(ss_ref, r1_ref, e1_ref, p2_ref, e2_ref):
    tf = ss_ref.shape[-1]
    K = PEER_TOPK
    ngroups = PEER_NKEYS // SUBLANES
    sub = lax.broadcasted_iota(jnp.int32, (SUBLANES, tf), 0)
    neg_inf = jnp.full((SUBLANES, tf), -jnp.inf, F32)

    def head_body(h, carry):
        a = _top16_sorted(ss_ref, 2 * h)
        b = _top16_sorted(ss_ref, 2 * h + 1)
        a_lo = a[0]
        a_hi = a[SUBLANES]
        for i in range(1, SUBLANES):
            a_lo = jnp.where(sub == i, a[i], a_lo)
            a_hi = jnp.where(sub == i, a[SUBLANES + i], a_hi)
        cand = [a_lo + b[j] for j in range(K)]
        extra = a_hi + b[0]
        ins = [jnp.maximum(cand[0], extra)]
        for j in range(1, K):
            ins.append(jnp.maximum(cand[j], jnp.minimum(cand[j - 1], extra)))
        top = _merge_across_sublanes(ins)
        thr = top[K - 1]
        smax = top[0]
        z = jnp.exp(top[0] - smax)
        for j in range(1, K):
            z = z + jnp.exp(top[j] - smax)
        inv_z = 1.0 / z

        r1_rows, e1_rows, p2_rows, e2_rows = [], [], [], []
        for i in range(ngroups):
            s1 = ss_ref[2 * h, SUBLANES * i:SUBLANES * (i + 1), :]
            s2 = ss_ref[2 * h + 1, SUBLANES * i:SUBLANES * (i + 1), :]
            rank1 = jnp.ones_like(s1)
            cap2 = jnp.zeros_like(s2)
            for r in range(K):
                rank1 = rank1 + jnp.where(a[r] > s1, 1.0, 0.0)
                cap2 = cap2 + jnp.where(a[r] + s2 >= thr, 1.0, 0.0)
            r1_rows.append(rank1)
            p2_rows.append(cap2)
            e1_rows.append(jnp.exp(s1 - a[0]))
            e2_rows.append(jnp.exp(s2 - b[0]) * inv_z)
        r1_ref[h] = jnp.concatenate(r1_rows, axis=0)
        e1_ref[h] = jnp.concatenate(e1_rows, axis=0)
        p2_ref[h] = jnp.concatenate(p2_rows, axis=0).astype(BF16)
        e2_ref[h] = jnp.concatenate(e2_rows, axis=0).astype(BF16)
        return carry

    del neg_inf
    lax.fori_loop(0, PEER_HEADS, head_body, 0)


def _route(ss, tf):
    nhp, nkeys, N = ss.shape
    spec_in = pl.BlockSpec((nhp, nkeys, tf), lambda t: (0, 0, t))
    spec_out = pl.BlockSpec((PEER_HEADS, nkeys, tf), lambda t: (0, 0, t))
    f32s = jax.ShapeDtypeStruct((PEER_HEADS, nkeys, N), F32)
    bf16s = jax.ShapeDtypeStruct((PEER_HEADS, nkeys, N), BF16)
    return pl.pallas_call(
        _route_kernel,
        grid=(N // tf,),
        in_specs=[spec_in],
        out_specs=[spec_out] * 4,
        out_shape=[f32s, f32s, bf16s, bf16s],
        compiler_params=_params("arbitrary"),
        name="route",
    )(ss)


def _peer_kernel(h2_ref, u_ref, vt_ref, r1_ref, e1_ref, p2_ref, e2_ref, x1_ref, mod_ref,
                 gpost_ref, o_ref, acc_ref):
    e = pl.program_id(2)
    n_e = pl.num_programs(2)
    eb = u_ref.shape[0]
    tg = h2_ref.shape[0]
    nkeys = PEER_NKEYS
    rb = eb // nkeys

    @pl.when(e == 0)
    def _():
        acc_ref[...] = jnp.zeros_like(acc_ref)

    scores = _dot_nt(u_ref[...], h2_ref[...])
    act = _gelu_tanh(scores.astype(BF16))
    zero = jnp.zeros((nkeys, tg), BF16)
    w_rows = []
    for j in range(rb):
        i1 = e * rb + j
        g = zero
        for h in range(PEER_HEADS):
            rank1 = jnp.broadcast_to(r1_ref[h, pl.ds(i1, 1), :], (nkeys, tg)).astype(BF16)
            gate1 = jnp.broadcast_to(e1_ref[h, pl.ds(i1, 1), :], (nkeys, tg)).astype(BF16)
            g = g + jnp.where(rank1 <= p2_ref[h], e2_ref[h], zero) * gate1
        w_rows.append(act[j * nkeys:(j + 1) * nkeys] * g)
    w = jnp.concatenate(w_rows, axis=0)
    acc_ref[...] += _dot(vt_ref[...], w)

    @pl.when(e == n_e - 1)
    def _():
        y = acc_ref[...].T
        gate2 = mod_ref[0, 5:6, :]
        o_ref[...] = x1_ref[...] + gate2 * _rms(y, gpost_ref[...])


def _peer(h2, u, vt, r1, e1, p2, e2, x1, mod3, g_post, B, S, tg, eb):
    N, D = h2.shape
    n_exp = u.shape[0]
    nblk = S // tg
    tok = lambda b, s, e: (b * nblk + s, 0)
    tab = lambda b, s, e: (0, 0, b * nblk + s)
    tab_spec = pl.BlockSpec((PEER_HEADS, PEER_NKEYS, tg), tab)
    return pl.pallas_call(
        _peer_kernel,
        grid=(B, nblk, n_exp // eb),
        in_specs=[pl.BlockSpec((tg, D), tok),
                  pl.BlockSpec((eb, D), lambda b, s, e: (e, 0)),
                  pl.BlockSpec((D, eb), lambda b, s, e: (0, e)),
                  tab_spec, tab_spec, tab_spec, tab_spec,
                  pl.BlockSpec((tg, D), tok),
                  pl.BlockSpec((1, 6, D), lambda b, s, e: (b, 0, 0)),
                  pl.BlockSpec((1, D), lambda b, s, e: (0, 0))],
        out_specs=pl.BlockSpec((tg, D), tok),
        out_shape=jax.ShapeDtypeStruct((N, D), F32),
        scratch_shapes=[pltpu.VMEM((D, tg), F32)],
        compiler_params=_params("arbitrary", "arbitrary", "arbitrary"),
        name="peer",
    )(h2, u, vt, r1, e1, p2, e2, x1, mod3, g_post)


def _block(n, cap):
    b = min(n, cap)
    assert n % b == 0, (n, b)
    return b


def _layer(x, c, w_ada, b_ada, norm_pre_mix, norm_post_mix, w_in, w_alpha, b_alpha, gla_norm,
           conv_w, conv_b, lru_w_a, lru_b_a, lru_w_x, lru_b_x, lru_lambda, w_out,
           norm_pre_ffn, norm_post_ffn, peer_w_query, peer_sub_keys, peer_u, peer_v):
    B, S, D = x.shape
    N = B * S
    dk = w_alpha.shape[1]
    rank = w_alpha.shape[0]
    assert S % GLA_CHUNK == 0 and D % 128 == 0

    o_q, o_k, o_v, o_r, o_al, o_xl, o_xg, o_ga, o_gb = (
        0, dk, 2 * dk, 2 * dk + D, 2 * dk + 2 * D, 2 * dk + 2 * D + rank,
        2 * dk + 3 * D + rank, 2 * dk + 4 * D + rank, 2 * dk + 5 * D + rank)
    w_main = jnp.concatenate(
        [w_in[:, o_q:o_r], w_in[:, o_r:o_al], w_in[:, o_xl:]], axis=1).astype(BF16)
    w_al = jnp.pad(w_in[:, o_al:o_xl], ((0, 0), (0, GLA_RANK_PAD - rank))).astype(BF16)
    w_alpha_p = jnp.pad(w_alpha, ((0, GLA_RANK_PAD - rank), (0, 0))).astype(BF16)
    row = lambda a: a.reshape(1, -1).astype(F32)

    mod3 = _ada(c.astype(F32), w_ada.astype(F32), b_ada.astype(F32)).reshape(B, 6, D)
    x2 = x.reshape(N, D)

    tb = _block(S, 512)
    qk, v, r, xl, xg, ga, gb, la = _inproj(
        x2, mod3, row(norm_pre_mix), w_main, w_al, w_alpha_p, row(b_alpha), B, S, tb)

    y_gla = _gla(qk, v, r, la, row(gla_norm), B, S, _block(S, 256))
    y_lru = _lru(xl, xg, conv_w.astype(F32), row(conv_b), lru_w_a.astype(BF16), row(lru_b_a),
                 lru_w_x.astype(BF16), row(lru_b_x), row(lru_lambda), B, S, _block(S, 256))

    nh, _, nkeys, half = peer_sub_keys.shape
    keys = peer_sub_keys.reshape(nh * 2, nkeys, half).astype(BF16)
    x1, h2, ss = _mix(y_gla, y_lru, ga, gb, x2, mod3, w_out.astype(BF16), row(norm_post_mix),
                      row(norm_pre_ffn), peer_w_query.astype(BF16), keys, B, S, _block(S, 512))

    r1, e1, p2, e2 = _route(ss, _block(N, 256))

    out = _peer(h2, peer_u.astype(BF16), peer_v.astype(BF16).T, r1, e1, p2, e2, x1, mod3,
                row(norm_post_ffn), B, S, _block(S, 512), 1024)
    return out.reshape(B, S, D)


def kernel(x, c, w_ada, b_ada, norm_pre_mix, norm_post_mix, w_in, w_alpha, b_alpha, gla_norm, conv_w, conv_b, lru_w_a, lru_b_a, lru_w_x, lru_b_x, lru_lambda, w_out, norm_pre_ffn, norm_post_ffn, peer_w_query, peer_sub_keys, peer_u, peer_v):
    depth = w_ada.shape[0]
    for l in range(depth):
        x = _layer(x, c, w_ada[l], b_ada[l], norm_pre_mix[l], norm_post_mix[l], w_in[l],
                   w_alpha[l], b_alpha[l], gla_norm[l], conv_w[l], conv_b[l], lru_w_a[l],
                   lru_b_a[l], lru_w_x[l], lru_b_x[l], lru_lambda[l], w_out[l],
                   norm_pre_ffn[l], norm_post_ffn[l], peer_w_query[l], peer_sub_keys[l],
                   peer_u[l], peer_v[l])
    return x
```

```python
import functools
import math

import jax
import jax.numpy as jnp
from jax import lax
from jax.experimental import pallas as pl
from jax.experimental.pallas import tpu as pltpu

F32 = jnp.float32
BF16 = jnp.bfloat16

EPS = 1e-6
GLA_HEADS = 4
GLA_RANK_PAD = 128
GLA_TAU = 16.0
GLA_CHUNK = 64
LRU_BLOCKS = 8
CONV_WIDTH = 4
LRU_C = 8.0
PEER_HEADS = 8
PEER_NKEYS = 128
PEER_TOPK = 16
PEER_SUB = 256
SUBLANES = 8

VMEM_LIMIT = 56 * 1024 * 1024

NT_DIMS = (((1,), (1,)), ((), ()))
TN_DIMS = (((0,), (0,)), ((), ()))


def _dot(a, b):
    return jnp.dot(a, b, preferred_element_type=F32)


def _dot_nt(a, b):
    return lax.dot_general(a, b, NT_DIMS, preferred_element_type=F32)


def _dot_tn(a, b):
    return lax.dot_general(a, b, TN_DIMS, preferred_element_type=F32)


def _sigmoid(x):
    return 1.0 / (1.0 + jnp.exp(-x))


def _gelu_tanh(x):
    c0 = math.sqrt(2.0 / math.pi)
    return 0.5 * x * (1.0 + jnp.tanh(c0 * (x + 0.044715 * (x * x * x))))


def _rms(x, gain):
    ms = jnp.mean(x * x, axis=-1, keepdims=True)
    return x * lax.rsqrt(ms + EPS) * gain


def _params(*sem, flags=None):
    return pltpu.CompilerParams(dimension_semantics=sem, vmem_limit_bytes=VMEM_LIMIT, flags=flags)


def _ada_kernel(c_ref, w_ref, b_ref, o_ref):
    c = c_ref[...]
    s = c * _sigmoid(c)
    o_ref[...] = jnp.dot(s, w_ref[...], precision=lax.Precision.HIGHEST,
                         preferred_element_type=F32) + b_ref[...]


def _ada(c, w_ada, b_ada):
    B, D = c.shape
    n_out = w_ada.shape[1]
    blk = 1024
    return pl.pallas_call(
        _ada_kernel,
        grid=(n_out // blk,),
        in_specs=[pl.BlockSpec((B, D), lambda j: (0, 0)),
                  pl.BlockSpec((D, blk), lambda j: (0, j)),
                  pl.BlockSpec((1, blk), lambda j: (0, j))],
        out_specs=pl.BlockSpec((B, blk), lambda j: (0, j)),
        out_shape=jax.ShapeDtypeStruct((B, n_out), F32),
        compiler_params=_params("arbitrary"),
        name="ada",
    )(c, w_ada, b_ada.reshape(1, n_out))


def _inproj_kernel(x_ref, mod_ref, gain_ref, w_ref, wal_ref, walpha_ref, balpha_ref,
                   qk_ref, v_ref, r_ref, xl_ref, xg_ref, ga_ref, gb_ref, la_ref):
    D = x_ref.shape[-1]
    x = x_ref[...]
    shift = mod_ref[0, 0:1, :]
    scale = mod_ref[0, 1:2, :]
    h = (_rms(x, gain_ref[...]) * (1.0 + scale) + shift).astype(BF16)
    for j, o_ref in enumerate((qk_ref, v_ref, r_ref, xl_ref, xg_ref, ga_ref, gb_ref)):
        o_ref[...] = _dot(h, w_ref[:, j * D:(j + 1) * D]).astype(BF16)
    a_low = _dot(h, wal_ref[...]).astype(BF16)
    z = _dot(a_low, walpha_ref[...]) + balpha_ref[...]
    log_sig = jnp.minimum(z, 0.0) - jnp.log(1.0 + jnp.exp(-jnp.abs(z)))
    la_ref[...] = log_sig * (1.0 / GLA_TAU)


def _inproj(x2, mod3, gain, w_main, w_al, w_alpha, b_alpha, B, S, tb):
    N, D = x2.shape
    dk = w_alpha.shape[1]
    nblk = S // tb
    tok = lambda b, s: (b * nblk + s, 0)
    const = lambda b, s: (0, 0)
    big = jax.ShapeDtypeStruct((N, D), BF16)
    return pl.pallas_call(
        _inproj_kernel,
        grid=(B, nblk),
        in_specs=[pl.BlockSpec((tb, D), tok),
                  pl.BlockSpec((1, 6, D), lambda b, s: (b, 0, 0)),
                  pl.BlockSpec((1, D), const),
                  pl.BlockSpec(w_main.shape, const, pipeline_mode=pl.Buffered(1)),
                  pl.BlockSpec(w_al.shape, const),
                  pl.BlockSpec(w_alpha.shape, const),
                  pl.BlockSpec((1, dk), const)],
        out_specs=[pl.BlockSpec((tb, D), tok)] * 7 + [pl.BlockSpec((tb, dk), tok)],
        out_shape=[big] * 7 + [jax.ShapeDtypeStruct((N, dk), F32)],
        compiler_params=_params("arbitrary", "arbitrary"),
        name="inproj",
    )(x2, mod3, gain, w_main, w_al, w_alpha, b_alpha)


def _gla_kernel(qk_ref, v_ref, r_ref, la_ref, gn_ref, tri_ref, o_ref, st_ref):
    tc = qk_ref.shape[0]
    dk_all = la_ref.shape[-1]
    hdk = dk_all // GLA_HEADS
    hdv = v_ref.shape[-1] // GLA_HEADS
    C = GLA_CHUNK
    mid = C // 2 - 1

    @pl.when(pl.program_id(1) == 0)
    def _():
        st_ref[...] = jnp.zeros_like(st_ref)

    la = la_ref[...]
    p0 = la.astype(BF16)
    rem = la - p0.astype(F32)
    p1 = rem.astype(BF16)
    p2 = (rem - p1.astype(F32)).astype(BF16)
    tri = tri_ref[...]
    cum = _dot(tri, p0) + _dot(tri, p1) + _dot(tri, p2)

    row = lax.broadcasted_iota(jnp.int32, (C, C), 0)
    col = lax.broadcasted_iota(jnp.int32, (C, C), 1)
    causal = row >= col
    q_scale = hdk ** -0.5

    for c in range(tc // C):
        rows = slice(c * C, (c + 1) * C)
        cumc = cum[rows]
        cm = cumc[mid:mid + 1]
        cl = cumc[C - 1:C]
        q = qk_ref[rows, 0:dk_all].astype(F32) * q_scale
        k = qk_ref[rows, dk_all:2 * dk_all].astype(F32)
        qt = q * jnp.exp(cumc - cm)
        kt = k * jnp.exp(cm - cumc)
        qi = (qt * jnp.exp(cm)).astype(BF16)
        kd = (kt * jnp.exp(cl - cm)).astype(BF16)
        e_last = jnp.exp(cl)
        qt = qt.astype(BF16)
        kt = kt.astype(BF16)
        for h in range(GLA_HEADS):
            ks = slice(h * hdk, (h + 1) * hdk)
            vs = slice(h * hdv, (h + 1) * hdv)
            scores = _dot_nt(qt[:, ks], kt[:, ks])
            scores = jnp.where(causal, scores, 0.0).astype(BF16)
            vh = v_ref[rows, vs]
            st = st_ref[h]
            o = _dot(scores, vh) + _dot_nt(qi[:, ks], st.astype(BF16))
            st_ref[h] = st * e_last[:, ks] + _dot_tn(vh, kd[:, ks])
            y = _rms(o, gn_ref[:, vs])
            rr = r_ref[rows, vs].astype(F32)
            o_ref[rows, vs] = (y * (rr * _sigmoid(rr))).astype(BF16)


def _gla(qk, v, r, la, gla_norm, B, S, tc):
    N, D = v.shape
    dk = la.shape[1]
    nblk = S // tc
    tok = lambda b, s: (b * nblk + s, 0)
    const = lambda b, s: (0, 0)
    idx = jnp.arange(tc)
    tri = ((idx[:, None] >= idx[None, :]) &
           (idx[:, None] // GLA_CHUNK == idx[None, :] // GLA_CHUNK)).astype(BF16)
    return pl.pallas_call(
        _gla_kernel,
        grid=(B, nblk),
        in_specs=[pl.BlockSpec((tc, 2 * dk), tok),
                  pl.BlockSpec((tc, D), tok),
                  pl.BlockSpec((tc, D), tok),
                  pl.BlockSpec((tc, dk), tok),
                  pl.BlockSpec((1, D), const),
                  pl.BlockSpec((tc, tc), const)],
        out_specs=pl.BlockSpec((tc, D), tok),
        out_shape=jax.ShapeDtypeStruct((N, D), BF16),
        scratch_shapes=[pltpu.VMEM((GLA_HEADS, D // GLA_HEADS, dk // GLA_HEADS), F32)],
        compiler_params=_params("arbitrary", "arbitrary"),
        name="gla",
    )(qk, v, r, la, gla_norm, tri)


def _lru_kernel(xl_ref, xg_ref, cw_ref, cb_ref, wa_ref, ba_ref, wx_ref, bx_ref, lam_ref,
                o_ref, xbuf_ref, h_ref):
    td, W = xl_ref.shape
    bw = W // LRU_BLOCKS
    pad = SUBLANES

    @pl.when(pl.program_id(1) == 0)
    def _():
        xbuf_ref[0:pad, :] = jnp.zeros((pad, W), F32)
        h_ref[...] = jnp.zeros_like(h_ref)

    xbuf_ref[pad:pad + td, :] = xl_ref[...].astype(F32)
    xc = cb_ref[...]
    for j in range(CONV_WIDTH):
        off = pad - (CONV_WIDTH - 1) + j
        xc = xc + cw_ref[j:j + 1, :] * xbuf_ref[off:off + td, :]
    xbuf_ref[0:pad, :] = xbuf_ref[td:td + pad, :]

    xcb = xc.astype(BF16)
    gr_parts, gi_parts = [], []
    for n in range(LRU_BLOCKS):
        blk = xcb[:, n * bw:(n + 1) * bw]
        gr_parts.append(_dot(blk, wa_ref[n]))
        gi_parts.append(_dot(blk, wx_ref[n]))
    gate_r = _sigmoid(jnp.concatenate(gr_parts, axis=1) + ba_ref[...])
    gate_i = _sigmoid(jnp.concatenate(gi_parts, axis=1) + bx_ref[...])
    neg_lam = -lam_ref[...]
    softplus = jnp.maximum(neg_lam, 0.0) + jnp.log(1.0 + jnp.exp(-jnp.abs(neg_lam)))
    log_a = (-LRU_C) * gate_r * softplus
    a = jnp.exp(log_a)
    u = jnp.sqrt(jnp.maximum(1.0 - jnp.exp(2.0 * log_a), 0.0)) * (gate_i * xc)

    row = lax.broadcasted_iota(jnp.int32, (td, W), 0) % SUBLANES
    d = 1
    while d < SUBLANES:
        keep = row >= d
        a_sh = pltpu.roll(a, d, 0)
        u_sh = pltpu.roll(u, d, 0)
        u = jnp.where(keep, a * u_sh + u, u)
        a = jnp.where(keep, a * a_sh, a)
        d *= 2
    carry = h_ref[0:1, :]
    groups = []
    for i in range(td // SUBLANES):
        rows = slice(i * SUBLANES, (i + 1) * SUBLANES)
        h_grp = u[rows] + a[rows] * carry
        carry = h_grp[SUBLANES - 1:SUBLANES, :]
        groups.append(h_grp)
    h = jnp.concatenate(groups, axis=0)
    h_ref[...] = jnp.broadcast_to(carry, h_ref.shape)
    o_ref[...] = (h * _gelu_tanh(xg_ref[...].astype(F32))).astype(BF16)


def _lru(xl, xg, conv_w, conv_b, wa, ba, wx, bx, lam, B, S, td):
    N, W = xl.shape
    nblk = S // td
    tok = lambda b, s: (b * nblk + s, 0)
    const2 = lambda b, s: (0, 0)
    const3 = lambda b, s: (0, 0, 0)
    return pl.pallas_call(
        _lru_kernel,
        grid=(B, nblk),
        in_specs=[pl.BlockSpec((td, W), tok),
                  pl.BlockSpec((td, W), tok),
                  pl.BlockSpec(conv_w.shape, const2),
                  pl.BlockSpec((1, W), const2),
                  pl.BlockSpec(wa.shape, const3),
                  pl.BlockSpec((1, W), const2),
                  pl.BlockSpec(wx.shape, const3),
                  pl.BlockSpec((1, W), const2),
                  pl.BlockSpec((1, W), const2)],
        out_specs=pl.BlockSpec((td, W), tok),
        out_shape=jax.ShapeDtypeStruct((N, W), BF16),
        scratch_shapes=[pltpu.VMEM((td + SUBLANES, W), F32),
                        pltpu.VMEM((SUBLANES, W), F32)],
        compiler_params=_params("arbitrary", "arbitrary"),
        name="lru",
    )(xl, xg, conv_w, conv_b, wa, ba, wx, bx, lam)


def _mix_kernel(yg_ref, yl_ref, ga_ref, gb_ref, x_ref, mod_ref, wout_ref, gpost_ref,
                gpre_ref, wq_ref, keys_ref, x1_ref, h2t_ref, ss_ref):
    merged = (_sigmoid(ga_ref[...].astype(F32)) * yg_ref[...].astype(F32)
              + _sigmoid(gb_ref[...].astype(F32)) * yl_ref[...].astype(F32))
    y = _dot(merged.astype(BF16), wout_ref[...])
    gate1 = mod_ref[0, 2:3, :]
    x1 = x_ref[...] + gate1 * _rms(y, gpost_ref[...])
    x1_ref[...] = x1
    shift2 = mod_ref[0, 3:4, :]
    scale2 = mod_ref[0, 4:5, :]
    h2f = _rms(x1, gpre_ref[...]) * (1.0 + scale2) + shift2
    h2 = h2f.astype(BF16)
    h2t_ref[...] = h2f.T.astype(BF16)
    qry = _dot(h2, wq_ref[...]).astype(BF16)
    half = keys_ref.shape[-1]
    for hp in range(keys_ref.shape[0]):
        ss_ref[hp] = _dot_nt(keys_ref[hp], qry[:, hp * half:(hp + 1) * half])


def _mix(yg, yl, ga, gb, x2, mod3, w_out, g_post, g_pre, w_q, keys, B, S, te):
    N, D = x2.shape
    nblk = S // te
    tok = lambda b, s: (b * nblk + s, 0)
    const2 = lambda b, s: (0, 0)
    nhp, nkeys, _ = keys.shape
    return pl.pallas_call(
        _mix_kernel,
        grid=(B, nblk),
        in_specs=[pl.BlockSpec((te, D), tok)] * 5 + [
            pl.BlockSpec((1, 6, D), lambda b, s: (b, 0, 0)),
            pl.BlockSpec(w_out.shape, const2),
            pl.BlockSpec((1, D), const2),
            pl.BlockSpec((1, D), const2),
            pl.BlockSpec(w_q.shape, const2),
            pl.BlockSpec(keys.shape, lambda b, s: (0, 0, 0))],
        out_specs=[pl.BlockSpec((te, D), tok),
                   pl.BlockSpec((D, te), lambda b, s: (0, b * nblk + s)),
                   pl.BlockSpec((nhp, nkeys, te), lambda b, s: (0, 0, b * nblk + s))],
        out_shape=[jax.ShapeDtypeStruct((N, D), F32),
                   jax.ShapeDtypeStruct((D, N), BF16),
                   jax.ShapeDtypeStruct((nhp, nkeys, N), F32)],
        compiler_params=_params("arbitrary", "arbitrary"),
        name="mix",
    )(yg, yl, ga, gb, x2, mod3, w_out, g_post, g_pre, w_q, keys)


def _oddeven_merge_sort_pairs(n):
    pairs = []
    p = 1
    while p < n:
        k = p
        while k >= 1:
            for j in range(k % p, n - k, 2 * k):
                for i in range(min(k, n - j - k)):
                    if (i + j) // (2 * p) == (i + j + k) // (2 * p):
                        pairs.append((i + j, i + j + k))
            k //= 2
        p *= 2
    return pairs


_SORT16 = _oddeven_merge_sort_pairs(PEER_TOPK)


def _sort_desc(vals):
    vals = list(vals)
    for i, j in _SORT16:
        hi = jnp.maximum(vals[i], vals[j])
        lo = jnp.minimum(vals[i], vals[j])
        vals[i], vals[j] = hi, lo
    return vals


def _bitonic_merge_desc(vals):
    vals = list(vals)
    n = len(vals)
    d = n // 2
    while d >= 1:
        for i in range(n):
            if (i & d) == 0:
                hi = jnp.maximum(vals[i], vals[i + d])
                lo = jnp.minimum(vals[i], vals[i + d])
                vals[i], vals[i + d] = hi, lo
        d //= 2
    return vals


def _merge_across_sublanes(vals):
    n = len(vals)
    for shift in (4, 2, 1):
        partner = [pltpu.roll(v, shift, 0) for v in vals]
        vals = _bitonic_merge_desc([jnp.maximum(vals[i], partner[n - 1 - i]) for i in range(n)])
    return vals


def _top16_sorted(s_ref, hp):
    groups = [s_ref[hp, SUBLANES * i:SUBLANES * (i + 1), :] for i in range(PEER_NKEYS // SUBLANES)]
    return _merge_across_sublanes(_sort_desc(groups))


def _dup_bf16_bits(x):
    hi = pltpu.bitcast(x.astype(BF16).astype(F32), jnp.uint32)
    return hi | lax.shift_right_logical(hi, jnp.full(hi.shape, 16, jnp.uint32))


def _route_kernel(ss_ref, r1_ref, e1_ref, p2_ref, e2_ref):
    tf = ss_ref.shape[-1]
    K = PEER_TOPK
    ngroups = PEER_NKEYS // SUBLANES
    sub = lax.broadcasted_iota(jnp.int32, (SUBLANES, tf), 0)

    def head_body(h, carry):
        a = _top16_sorted(ss_ref, 2 * h)
        b = _top16_sorted(ss_ref, 2 * h + 1)
        a_lo = a[0]
        a_hi = a[SUBLANES]
        for i in range(1, SUBLANES):
            a_lo = jnp.where(sub == i, a[i], a_lo)
            a_hi = jnp.where(sub == i, a[SUBLANES + i], a_hi)
        cand = [a_lo + b[j] for j in range(K)]
        extra = a_hi + b[0]
        ins = [jnp.maximum(cand[0], extra)]
        for j in range(1, K):
            ins.append(jnp.maximum(cand[j], jnp.minimum(cand[j - 1], extra)))
        top = _merge_across_sublanes(ins)
        thr = top[K - 1]
        smax = top[0]
        z = jnp.exp(top[0] - smax)
        for j in range(1, K):
            z = z + jnp.exp(top[j] - smax)
        inv_z = 1.0 / z

        cap = []
        for j in range(K):
            cnt = (jnp.where(cand[j] >= thr, 1.0, 0.0)
                   + jnp.where(a_hi + b[j] >= thr, 1.0, 0.0))
            for shift in (4, 2, 1):
                cnt = cnt + pltpu.roll(cnt, shift, 0)
            cap.append(cnt)

        r1_rows, e1_rows, p2_rows, e2_rows = [], [], [], []
        for i in range(ngroups):
            s1 = ss_ref[2 * h, SUBLANES * i:SUBLANES * (i + 1), :]
            s2 = ss_ref[2 * h + 1, SUBLANES * i:SUBLANES * (i + 1), :]
            rank1 = jnp.full_like(s1, K + 1.0)
            cap2 = jnp.zeros_like(s2)
            for r in reversed(range(K)):
                rank1 = jnp.where(s1 == a[r], r + 1.0, rank1)
                cap2 = jnp.where(s2 == b[r], cap[r], cap2)
            r1_rows.append(rank1)
            p2_rows.append(cap2)
            e1_rows.append(jnp.exp(s1 - a[0]))
            e2_rows.append(jnp.exp(s2 - b[0]) * (0.5 * inv_z))
        r1_ref[h] = _dup_bf16_bits(jnp.concatenate(r1_rows, axis=0))
        e1_ref[h] = _dup_bf16_bits(jnp.concatenate(e1_rows, axis=0))
        p2_ref[h] = jnp.concatenate(p2_rows, axis=0).astype(BF16)
        e2_ref[h] = jnp.concatenate(e2_rows, axis=0).astype(BF16)
        return carry

    lax.fori_loop(0, PEER_HEADS, head_body, 0)


def _route(ss, tf):
    nhp, nkeys, N = ss.shape
    spec_in = pl.BlockSpec((nhp, nkeys, tf), lambda t: (0, 0, t))
    spec_out = pl.BlockSpec((PEER_HEADS, nkeys, tf), lambda t: (0, 0, t))
    f32s = jax.ShapeDtypeStruct((PEER_HEADS, nkeys, N), jnp.uint32)
    bf16s = jax.ShapeDtypeStruct((PEER_HEADS, nkeys, N), BF16)
    return pl.pallas_call(
        _route_kernel,
        grid=(N // tf,),
        in_specs=[spec_in],
        out_specs=[spec_out] * 4,
        out_shape=[f32s, f32s, bf16s, bf16s],
        compiler_params=_params("arbitrary"),
        name="route",
    )(ss)


def _peer_kernel(h2t_ref, u_ref, vt_ref, r1_ref, e1_ref, p2_ref, e2_ref, x1_ref, mod_ref,
                 gpost_ref, o_ref, acc_ref, xa_ref, xb_ref, wa_ref, wb_ref):
    e = pl.program_id(2)
    n_e = pl.num_programs(2)
    n_sub = u_ref.shape[0]
    tg = h2t_ref.shape[1]
    rows_per_sub = PEER_SUB // PEER_NKEYS
    chunk = 2 * SUBLANES
    n_chunks = PEER_NKEYS // chunk
    c0 = math.sqrt(2.0 / math.pi)

    @pl.when(e == 0)
    def _():
        acc_ref[...] = jnp.zeros_like(acc_ref)

    xbufs = (xa_ref, xb_ref)
    wbufs = (wa_ref, wb_ref)

    def scores_into(sb, par):
        xbufs[par][...] = _dot(u_ref[sb], h2t_ref[...]).astype(BF16)

    def stack(parts):
        return jnp.concatenate(parts, axis=0)

    def gates(sb, par):
        xbuf_ref = xbufs[par]
        wbuf_ref = wbufs[par]
        i1 = sb * rows_per_sub

        def row_bcast(ref, h, lanes):
            return stack([pltpu.bitcast(
                jnp.broadcast_to(ref[h, i1 + j:i1 + j + 1, lanes], (SUBLANES, 128)), BF16)
                for j in range(rows_per_sub)])

        zero = jnp.zeros((rows_per_sub * chunk, 128), BF16)
        for lg in range(tg // 128):
            lanes = slice(lg * 128, (lg + 1) * 128)
            g = [zero] * n_chunks
            for h in range(PEER_HEADS):
                rank1 = row_bcast(r1_ref, h, lanes)
                gate1 = row_bcast(e1_ref, h, lanes)
                for c in range(n_chunks):
                    keys = slice(c * chunk, (c + 1) * chunk)
                    cap2 = stack([p2_ref[h, keys, lanes]] * rows_per_sub)
                    gate2 = stack([e2_ref[h, keys, lanes]] * rows_per_sub)
                    g[c] = g[c] + jnp.where(rank1 <= cap2, gate2, 0.0) * gate1
            for c in range(n_chunks):
                x = stack([xbuf_ref[j * PEER_NKEYS + c * chunk:j * PEER_NKEYS + (c + 1) * chunk, lanes]
                           for j in range(rows_per_sub)])
                w = x * (1.0 + jnp.tanh(x * (c0 + (c0 * 0.044715) * (x * x)))) * g[c]
                for j in range(rows_per_sub):
                    wbuf_ref[j * PEER_NKEYS + c * chunk:j * PEER_NKEYS + (c + 1) * chunk, lanes] = (
                        w[j * chunk:(j + 1) * chunk])

    def accumulate(sb, par):
        acc_ref[...] += _dot(vt_ref[sb], wbufs[par][...])

    scores_into(0, 0)
    for k in range(n_sub):
        if k + 1 < n_sub:
            scores_into(k + 1, (k + 1) % 2)
        if k >= 1:
            accumulate(k - 1, (k - 1) % 2)
        gates(k, k % 2)
    accumulate(n_sub - 1, (n_sub - 1) % 2)

    @pl.when(e == n_e - 1)
    def _():
        y = acc_ref[...].T
        gate2 = mod_ref[0, 5:6, :]
        o_ref[...] = x1_ref[...] + gate2 * _rms(y, gpost_ref[...])


def _peer(h2t, u3, vt3, r1, e1, p2, e2, x1, mod3, g_post, B, S, tg, n_sub):
    D, N = h2t.shape
    n_sub_total = u3.shape[0]
    nblk = S // tg
    tok = lambda b, s, e: (b * nblk + s, 0)
    tab_spec = pl.BlockSpec((PEER_HEADS, PEER_NKEYS, tg), lambda b, s, e: (0, 0, b * nblk + s))
    rows_per_step = n_sub * PEER_SUB // PEER_NKEYS
    row_spec = pl.BlockSpec((PEER_HEADS, rows_per_step, tg), lambda b, s, e: (0, e, b * nblk + s))
    return pl.pallas_call(
        _peer_kernel,
        grid=(B, nblk, n_sub_total // n_sub),
        in_specs=[pl.BlockSpec((D, tg), lambda b, s, e: (0, b * nblk + s)),
                  pl.BlockSpec((n_sub, PEER_SUB, D), lambda b, s, e: (e, 0, 0)),
                  pl.BlockSpec((n_sub, D, PEER_SUB), lambda b, s, e: (e, 0, 0)),
                  row_spec, row_spec, tab_spec, tab_spec,
                  pl.BlockSpec((tg, D), tok),
                  pl.BlockSpec((1, 6, D), lambda b, s, e: (b, 0, 0)),
                  pl.BlockSpec((1, D), lambda b, s, e: (0, 0))],
        out_specs=pl.BlockSpec((tg, D), tok),
        out_shape=jax.ShapeDtypeStruct((N, D), F32),
        scratch_shapes=[pltpu.VMEM((D, tg), F32),
                        pltpu.VMEM((PEER_SUB, tg), BF16),
                        pltpu.VMEM((PEER_SUB, tg), BF16),
                        pltpu.VMEM((PEER_SUB, tg), BF16),
                        pltpu.VMEM((PEER_SUB, tg), BF16)],
        compiler_params=_params("arbitrary", "arbitrary", "arbitrary"),
        name="peer",
    )(h2t, u3, vt3, r1, e1, p2, e2, x1, mod3, g_post)


def _block(n, cap):
    b = min(n, cap)
    assert n % b == 0, (n, b)
    return b


def _layer(x, c, w_ada, b_ada, norm_pre_mix, norm_post_mix, w_in, w_alpha, b_alpha, gla_norm,
           conv_w, conv_b, lru_w_a, lru_b_a, lru_w_x, lru_b_x, lru_lambda, w_out,
           norm_pre_ffn, norm_post_ffn, peer_w_query, peer_sub_keys, peer_u, peer_v):
    B, S, D = x.shape
    N = B * S
    dk = w_alpha.shape[1]
    rank = w_alpha.shape[0]
    assert S % GLA_CHUNK == 0 and D % 128 == 0

    o_q, o_k, o_v, o_r, o_al, o_xl, o_xg, o_ga, o_gb = (
        0, dk, 2 * dk, 2 * dk + D, 2 * dk + 2 * D, 2 * dk + 2 * D + rank,
        2 * dk + 3 * D + rank, 2 * dk + 4 * D + rank, 2 * dk + 5 * D + rank)
    w_main = jnp.concatenate(
        [w_in[:, o_q:o_r], w_in[:, o_r:o_al], w_in[:, o_xl:]], axis=1).astype(BF16)
    w_al = jnp.pad(w_in[:, o_al:o_xl], ((0, 0), (0, GLA_RANK_PAD - rank))).astype(BF16)
    w_alpha_p = jnp.pad(w_alpha, ((0, GLA_RANK_PAD - rank), (0, 0))).astype(BF16)
    row = lambda a: a.reshape(1, -1).astype(F32)

    mod3 = _ada(c.astype(F32), w_ada.astype(F32), b_ada.astype(F32)).reshape(B, 6, D)
    x2 = x.reshape(N, D)

    tb = _block(S, 512)
    qk, v, r, xl, xg, ga, gb, la = _inproj(
        x2, mod3, row(norm_pre_mix), w_main, w_al, w_alpha_p, row(b_alpha), B, S, tb)

    y_gla = _gla(qk, v, r, la, row(gla_norm), B, S, _block(S, 256))
    y_lru = _lru(xl, xg, conv_w.astype(F32), row(conv_b), lru_w_a.astype(BF16), row(lru_b_a),
                 lru_w_x.astype(BF16), row(lru_b_x), row(lru_lambda), B, S, _block(S, 256))

    nh, _, nkeys, half = peer_sub_keys.shape
    keys = peer_sub_keys.reshape(nh * 2, nkeys, half).astype(BF16)
    x1, h2t, ss = _mix(y_gla, y_lru, ga, gb, x2, mod3, w_out.astype(BF16), row(norm_post_mix),
                      row(norm_pre_ffn), peer_w_query.astype(BF16), keys, B, S, _block(S, 512))

    r1, e1, p2, e2 = _route(ss, _block(N, 256))

    n_exp = peer_u.shape[0]
    u3 = peer_u.astype(BF16).reshape(n_exp // PEER_SUB, PEER_SUB, D)
    vt3 = peer_v.astype(BF16).reshape(n_exp // PEER_SUB, PEER_SUB, D).transpose(0, 2, 1)
    out = _peer(h2t, u3, vt3, r1, e1, p2, e2, x1, mod3, row(norm_post_ffn), B, S,
                _block(S, 512), 8)
    return out.reshape(B, S, D)


def kernel(x, c, w_ada, b_ada, norm_pre_mix, norm_post_mix, w_in, w_alpha, b_alpha, gla_norm, conv_w, conv_b, lru_w_a, lru_b_a, lru_w_x, lru_b_x, lru_lambda, w_out, norm_pre_ffn, norm_post_ffn, peer_w_query, peer_sub_keys, peer_u, peer_v):
    depth = w_ada.shape[0]
    for l in range(depth):
        x = _layer(x, c, w_ada[l], b_ada[l], norm_pre_mix[l], norm_post_mix[l], w_in[l],
                   w_alpha[l], b_alpha[l], gla_norm[l], conv_w[l], conv_b[l], lru_w_a[l],
                   lru_b_a[l], lru_w_x[l], lru_b_x[l], lru_lambda[l], w_out[l],
                   norm_pre_ffn[l], norm_post_ffn[l], peer_w_query[l], peer_sub_keys[l],
                   peer_u[l], peer_v[l])
    return x
```

```python
import functools
import math

import jax
import jax.numpy as jnp
from jax import lax
from jax.experimental import pallas as pl
from jax.experimental.pallas import tpu as pltpu

F32 = jnp.float32
BF16 = jnp.bfloat16

EPS = 1e-6
GLA_HEADS = 4
GLA_RANK_PAD = 128
GLA_TAU = 16.0
GLA_CHUNK = 128
LRU_BLOCKS = 8
CONV_WIDTH = 4
LRU_C = 8.0
PEER_HEADS = 8
PEER_NKEYS = 128
PEER_TOPK = 16
PEER_SUB = 256
PEER_UNIT_TOKENS = 256
SUBLANES = 8

VMEM_LIMIT = 56 * 1024 * 1024

NT_DIMS = (((1,), (1,)), ((), ()))
TN_DIMS = (((0,), (0,)), ((), ()))


def _dot(a, b):
    return jnp.dot(a, b, preferred_element_type=F32)


def _dot_nt(a, b):
    return lax.dot_general(a, b, NT_DIMS, preferred_element_type=F32)


def _dot_tn(a, b):
    return lax.dot_general(a, b, TN_DIMS, preferred_element_type=F32)


def _sigmoid(x):
    return 1.0 / (1.0 + jnp.exp(-x))


def _gelu_tanh(x):
    c0 = math.sqrt(2.0 / math.pi)
    return 0.5 * x * (1.0 + jnp.tanh(c0 * (x + 0.044715 * (x * x * x))))


def _rms(x, gain):
    ms = jnp.mean(x * x, axis=-1, keepdims=True)
    return x * lax.rsqrt(ms + EPS) * gain


def _params(*sem, flags=None):
    return pltpu.CompilerParams(dimension_semantics=sem, vmem_limit_bytes=VMEM_LIMIT, flags=flags)


def _ada_kernel(c_ref, w_ref, b_ref, o_ref):
    c = c_ref[...]
    s = c * _sigmoid(c)
    o_ref[...] = jnp.dot(s, w_ref[...], precision=lax.Precision.HIGHEST,
                         preferred_element_type=F32) + b_ref[...]


def _ada(c, w_ada, b_ada):
    B, D = c.shape
    n_out = w_ada.shape[1]
    blk = 1024
    return pl.pallas_call(
        _ada_kernel,
        grid=(n_out // blk,),
        in_specs=[pl.BlockSpec((B, D), lambda j: (0, 0)),
                  pl.BlockSpec((D, blk), lambda j: (0, j)),
                  pl.BlockSpec((1, blk), lambda j: (0, j))],
        out_specs=pl.BlockSpec((B, blk), lambda j: (0, j)),
        out_shape=jax.ShapeDtypeStruct((B, n_out), F32),
        compiler_params=_params("arbitrary"),
        name="ada",
    )(c, w_ada, b_ada.reshape(1, n_out))


def _inproj_kernel(x_ref, mod_ref, gain_ref, w_ref, wal_ref, walpha_ref, balpha_ref,
                   qk_ref, v_ref, r_ref, xl_ref, xg_ref, ga_ref, gb_ref, la_ref):
    D = x_ref.shape[-1]
    x = x_ref[...]
    shift = mod_ref[0, 0:1, :]
    scale = mod_ref[0, 1:2, :]
    h = (_rms(x, gain_ref[...]) * (1.0 + scale) + shift).astype(BF16)
    for j, o_ref in enumerate((qk_ref, v_ref, r_ref, xl_ref, xg_ref, ga_ref, gb_ref)):
        o_ref[...] = _dot(h, w_ref[:, j * D:(j + 1) * D]).astype(BF16)
    a_low = _dot(h, wal_ref[...]).astype(BF16)
    z = _dot(a_low, walpha_ref[...]) + balpha_ref[...]
    log_sig = jnp.minimum(z, 0.0) - jnp.log(1.0 + jnp.exp(-jnp.abs(z)))
    la_ref[...] = log_sig * (1.0 / GLA_TAU)


def _inproj(x2, mod3, gain, w_main, w_al, w_alpha, b_alpha, B, S, tb):
    N, D = x2.shape
    dk = w_alpha.shape[1]
    nblk = S // tb
    tok = lambda b, s: (b * nblk + s, 0)
    const = lambda b, s: (0, 0)
    big = jax.ShapeDtypeStruct((N, D), BF16)
    return pl.pallas_call(
        _inproj_kernel,
        grid=(B, nblk),
        in_specs=[pl.BlockSpec((tb, D), tok),
                  pl.BlockSpec((1, 6, D), lambda b, s: (b, 0, 0)),
                  pl.BlockSpec((1, D), const),
                  pl.BlockSpec(w_main.shape, const, pipeline_mode=pl.Buffered(1)),
                  pl.BlockSpec(w_al.shape, const),
                  pl.BlockSpec(w_alpha.shape, const),
                  pl.BlockSpec((1, dk), const)],
        out_specs=[pl.BlockSpec((tb, D), tok)] * 7 + [pl.BlockSpec((tb, dk), tok)],
        out_shape=[big] * 7 + [jax.ShapeDtypeStruct((N, dk), F32)],
        compiler_params=_params("arbitrary", "arbitrary"),
        name="inproj",
    )(x2, mod3, gain, w_main, w_al, w_alpha, b_alpha)


def _gla_kernel(qk_ref, v_ref, r_ref, la_ref, gn_ref, tri_ref, o_ref, st_ref):
    tc = qk_ref.shape[0]
    dk_all = la_ref.shape[-1]
    hdk = dk_all // GLA_HEADS
    hdv = v_ref.shape[-1] // GLA_HEADS
    C = GLA_CHUNK
    n_c = tc // C
    mid = C // 2 - 1

    @pl.when(pl.program_id(1) == 0)
    def _():
        st_ref[...] = jnp.zeros_like(st_ref)

    la = la_ref[...]
    p0 = la.astype(BF16)
    rem = la - p0.astype(F32)
    p1 = rem.astype(BF16)
    p2 = (rem - p1.astype(F32)).astype(BF16)
    tri = tri_ref[...]
    cum = _dot(tri, p0) + _dot(tri, p1) + _dot(tri, p2)

    def per_chunk_row(r):
        return jnp.concatenate(
            [jnp.broadcast_to(cum[c * C + r:c * C + r + 1], (C, dk_all)) for c in range(n_c)], axis=0)

    cm = per_chunk_row(mid)
    cl = per_chunk_row(C - 1)
    qt = qk_ref[:, 0:dk_all].astype(F32) * (hdk ** -0.5) * jnp.exp(cum - cm)
    kt = qk_ref[:, dk_all:2 * dk_all].astype(F32) * jnp.exp(cm - cum)
    qi = (qt * jnp.exp(cm)).astype(BF16)
    kd = (kt * jnp.exp(cl - cm)).astype(BF16)
    qt = qt.astype(BF16)
    kt = kt.astype(BF16)

    row = lax.broadcasted_iota(jnp.int32, (C, C), 0)
    col = lax.broadcasted_iota(jnp.int32, (C, C), 1)
    causal = row >= col
    pairs = [(c, h) for c in range(n_c) for h in range(GLA_HEADS)]
    rows = lambda c: slice(c * C, (c + 1) * C)
    ks = lambda h: slice(h * hdk, (h + 1) * hdk)
    vs = lambda h: slice(h * hdv, (h + 1) * hdv)

    scores = {p: jnp.where(causal, _dot_nt(qt[rows(p[0]), ks(p[1])], kt[rows(p[0]), ks(p[1])]),
                           0.0).astype(BF16) for p in pairs}
    intra = {p: _dot(scores[p], v_ref[rows(p[0]), vs(p[1])]) for p in pairs}
    update = {p: _dot_tn(v_ref[rows(p[0]), vs(p[1])], kd[rows(p[0]), ks(p[1])]) for p in pairs}
    state_in = {}
    for h in range(GLA_HEADS):
        st = st_ref[h]
        for c in range(n_c):
            state_in[(c, h)] = st.astype(BF16)
            e_last = jnp.exp(cum[(c + 1) * C - 1:(c + 1) * C, ks(h)])
            st = st * e_last + update[(c, h)]
        st_ref[h] = st
    for c, h in pairs:
        o = intra[(c, h)] + _dot_nt(qi[rows(c), ks(h)], state_in[(c, h)])
        y = _rms(o, gn_ref[:, vs(h)])
        rr = r_ref[rows(c), vs(h)].astype(F32)
        o_ref[rows(c), vs(h)] = (y * (rr * _sigmoid(rr))).astype(BF16)


def _gla(qk, v, r, la, gla_norm, B, S, tc):
    N, D = v.shape
    dk = la.shape[1]
    nblk = S // tc
    tok = lambda b, s: (b * nblk + s, 0)
    const = lambda b, s: (0, 0)
    idx = jnp.arange(tc)
    tri = ((idx[:, None] >= idx[None, :]) &
           (idx[:, None] // GLA_CHUNK == idx[None, :] // GLA_CHUNK)).astype(BF16)
    return pl.pallas_call(
        _gla_kernel,
        grid=(B, nblk),
        in_specs=[pl.BlockSpec((tc, 2 * dk), tok),
                  pl.BlockSpec((tc, D), tok),
                  pl.BlockSpec((tc, D), tok),
                  pl.BlockSpec((tc, dk), tok),
                  pl.BlockSpec((1, D), const),
                  pl.BlockSpec((tc, tc), const)],
        out_specs=pl.BlockSpec((tc, D), tok),
        out_shape=jax.ShapeDtypeStruct((N, D), BF16),
        scratch_shapes=[pltpu.VMEM((GLA_HEADS, D // GLA_HEADS, dk // GLA_HEADS), F32)],
        compiler_params=_params("arbitrary", "arbitrary"),
        name="gla",
    )(qk, v, r, la, gla_norm, tri)


def _lru_kernel(xl_ref, xg_ref, perm_ref, cw_ref, cb_ref, wa_ref, ba_ref, wx_ref, bx_ref,
                lam_ref, o_ref, xbuf_ref, h_ref):
    nb, tt, W = xl_ref.shape
    rows = nb * tt
    bw = W // LRU_BLOCKS
    tail = (CONV_WIDTH - 1) * nb

    @pl.when(pl.program_id(0) == 0)
    def _():
        xbuf_ref[0:tail, :] = jnp.zeros((tail, W), F32)
        h_ref[...] = jnp.zeros_like(h_ref)

    perm = perm_ref[...]
    xl = _dot(perm, xl_ref[...].reshape(rows, W))
    xg = _dot(perm, xg_ref[...].reshape(rows, W))

    xbuf_ref[tail:tail + rows, :] = xl
    xc = cb_ref[...]
    for j in range(CONV_WIDTH):
        xc = xc + cw_ref[j:j + 1, :] * xbuf_ref[j * nb:j * nb + rows, :]
    xbuf_ref[0:tail, :] = xbuf_ref[rows:rows + tail, :]

    xcb = xc.astype(BF16)
    gr_parts, gi_parts = [], []
    for n in range(LRU_BLOCKS):
        blk = xcb[:, n * bw:(n + 1) * bw]
        gr_parts.append(_dot(blk, wa_ref[n]))
        gi_parts.append(_dot(blk, wx_ref[n]))
    gate_r = _sigmoid(jnp.concatenate(gr_parts, axis=1) + ba_ref[...])
    gate_i = _sigmoid(jnp.concatenate(gi_parts, axis=1) + bx_ref[...])
    neg_lam = -lam_ref[...]
    softplus = jnp.maximum(neg_lam, 0.0) + jnp.log(1.0 + jnp.exp(-jnp.abs(neg_lam)))
    log_a = (-LRU_C) * gate_r * softplus
    a = jnp.exp(log_a)
    u = jnp.sqrt(jnp.maximum(1.0 - jnp.exp(2.0 * log_a), 0.0)) * (gate_i * xc)

    h = h_ref[...]
    steps = []
    for t in range(tt):
        grp = slice(t * nb, (t + 1) * nb)
        h = a[grp] * h + u[grp]
        steps.append(h)
    h_ref[...] = h
    y = (jnp.concatenate(steps, axis=0) * _gelu_tanh(xg)).astype(BF16)
    y = lax.dot_general(perm, y, TN_DIMS, preferred_element_type=F32).astype(BF16)
    o_ref[...] = y.reshape(nb, tt, W)


def _lru(xl, xg, conv_w, conv_b, wa, ba, wx, bx, lam, B, S, tt):
    N, W = xl.shape
    assert B == SUBLANES, "the time-major layout places the batch on the sublane axis"
    rows = B * tt
    r = jnp.arange(rows)
    perm = (r[None, :] == (r[:, None] % B) * tt + r[:, None] // B).astype(BF16)
    blk = lambda s: (0, s, 0)
    const2 = lambda s: (0, 0)
    const3 = lambda s: (0, 0, 0)
    out = pl.pallas_call(
        _lru_kernel,
        grid=(S // tt,),
        in_specs=[pl.BlockSpec((B, tt, W), blk),
                  pl.BlockSpec((B, tt, W), blk),
                  pl.BlockSpec((rows, rows), const2),
                  pl.BlockSpec(conv_w.shape, const2),
                  pl.BlockSpec((1, W), const2),
                  pl.BlockSpec(wa.shape, const3),
                  pl.BlockSpec((1, W), const2),
                  pl.BlockSpec(wx.shape, const3),
                  pl.BlockSpec((1, W), const2),
                  pl.BlockSpec((1, W), const2)],
        out_specs=pl.BlockSpec((B, tt, W), blk),
        out_shape=jax.ShapeDtypeStruct((B, S, W), BF16),
        scratch_shapes=[pltpu.VMEM((rows + (CONV_WIDTH - 1) * B, W), F32),
                        pltpu.VMEM((B, W), F32)],
        compiler_params=_params("arbitrary"),
        name="lru",
    )(xl.reshape(B, S, W), xg.reshape(B, S, W), perm, conv_w, conv_b, wa, ba, wx, bx, lam)
    return out.reshape(N, W)


def _mix_kernel(yg_ref, yl_ref, ga_ref, gb_ref, x_ref, mod_ref, wout_ref, gpost_ref,
                gpre_ref, wq_ref, keys_ref, x1_ref, h2t_ref, ss_ref):
    merged = (_sigmoid(ga_ref[...].astype(F32)) * yg_ref[...].astype(F32)
              + _sigmoid(gb_ref[...].astype(F32)) * yl_ref[...].astype(F32))
    y = _dot(merged.astype(BF16), wout_ref[...])
    gate1 = mod_ref[0, 2:3, :]
    x1 = x_ref[...] + gate1 * _rms(y, gpost_ref[...])
    x1_ref[...] = x1
    shift2 = mod_ref[0, 3:4, :]
    scale2 = mod_ref[0, 4:5, :]
    h2f = _rms(x1, gpre_ref[...]) * (1.0 + scale2) + shift2
    h2 = h2f.astype(BF16)
    h2t_ref[...] = h2f.T.astype(BF16)
    qry = _dot(h2, wq_ref[...]).astype(BF16)
    half = keys_ref.shape[-1]
    for hp in range(keys_ref.shape[0]):
        ss_ref[hp] = _dot_nt(keys_ref[hp], qry[:, hp * half:(hp + 1) * half])


def _mix(yg, yl, ga, gb, x2, mod3, w_out, g_post, g_pre, w_q, keys, B, S, te):
    N, D = x2.shape
    nblk = S // te
    tok = lambda b, s: (b * nblk + s, 0)
    const2 = lambda b, s: (0, 0)
    nhp, nkeys, _ = keys.shape
    return pl.pallas_call(
        _mix_kernel,
        grid=(B, nblk),
        in_specs=[pl.BlockSpec((te, D), tok)] * 5 + [
            pl.BlockSpec((1, 6, D), lambda b, s: (b, 0, 0)),
            pl.BlockSpec(w_out.shape, const2),
            pl.BlockSpec((1, D), const2),
            pl.BlockSpec((1, D), const2),
            pl.BlockSpec(w_q.shape, const2),
            pl.BlockSpec(keys.shape, lambda b, s: (0, 0, 0))],
        out_specs=[pl.BlockSpec((te, D), tok),
                   pl.BlockSpec((D, te), lambda b, s: (0, b * nblk + s)),
                   pl.BlockSpec((nhp, nkeys, te), lambda b, s: (0, 0, b * nblk + s))],
        out_shape=[jax.ShapeDtypeStruct((N, D), F32),
                   jax.ShapeDtypeStruct((D, N), BF16),
                   jax.ShapeDtypeStruct((nhp, nkeys, N), F32)],
        compiler_params=_params("arbitrary", "arbitrary"),
        name="mix",
    )(yg, yl, ga, gb, x2, mod3, w_out, g_post, g_pre, w_q, keys)


def _oddeven_merge_sort_pairs(n):
    pairs = []
    p = 1
    while p < n:
        k = p
        while k >= 1:
            for j in range(k % p, n - k, 2 * k):
                for i in range(min(k, n - j - k)):
                    if (i + j) // (2 * p) == (i + j + k) // (2 * p):
                        pairs.append((i + j, i + j + k))
            k //= 2
        p *= 2
    return pairs


_SORT16 = _oddeven_merge_sort_pairs(PEER_TOPK)


def _sort_desc(vals):
    vals = list(vals)
    for i, j in _SORT16:
        hi = jnp.maximum(vals[i], vals[j])
        lo = jnp.minimum(vals[i], vals[j])
        vals[i], vals[j] = hi, lo
    return vals


def _bitonic_merge_desc(vals):
    vals = list(vals)
    n = len(vals)
    d = n // 2
    while d >= 1:
        for i in range(n):
            if (i & d) == 0:
                hi = jnp.maximum(vals[i], vals[i + d])
                lo = jnp.minimum(vals[i], vals[i + d])
                vals[i], vals[i + d] = hi, lo
        d //= 2
    return vals


def _merge_across_sublanes(vals):
    n = len(vals)
    for shift in (4, 2, 1):
        partner = [pltpu.roll(v, shift, 0) for v in vals]
        vals = _bitonic_merge_desc([jnp.maximum(vals[i], partner[n - 1 - i]) for i in range(n)])
    return vals


def _top16_sorted(s_ref, hp):
    groups = [s_ref[hp, SUBLANES * i:SUBLANES * (i + 1), :] for i in range(PEER_NKEYS // SUBLANES)]
    return _merge_across_sublanes(_sort_desc(groups))


def _dup_bf16_bits(x):
    hi = pltpu.bitcast(x.astype(BF16).astype(F32), jnp.uint32)
    return hi | lax.shift_right_logical(hi, jnp.full(hi.shape, 16, jnp.uint32))


def _route_kernel(ss_ref, r1_ref, e1_ref, p2_ref, e2_ref):
    tf = ss_ref.shape[-1]
    K = PEER_TOPK
    ngroups = PEER_NKEYS // SUBLANES
    sub = lax.broadcasted_iota(jnp.int32, (SUBLANES, tf), 0)

    def head_body(h, carry):
        a = _top16_sorted(ss_ref, 2 * h)
        b = _top16_sorted(ss_ref, 2 * h + 1)
        a_lo = a[0]
        a_hi = a[SUBLANES]
        for i in range(1, SUBLANES):
            a_lo = jnp.where(sub == i, a[i], a_lo)
            a_hi = jnp.where(sub == i, a[SUBLANES + i], a_hi)
        cand = [a_lo + b[j] for j in range(K)]
        extra = a_hi + b[0]
        ins = [jnp.maximum(cand[0], extra)]
        for j in range(1, K):
            ins.append(jnp.maximum(cand[j], jnp.minimum(cand[j - 1], extra)))
        top = _merge_across_sublanes(ins)
        thr = top[K - 1]
        smax = top[0]
        z = jnp.exp(top[0] - smax)
        for j in range(1, K):
            z = z + jnp.exp(top[j] - smax)
        inv_z = 1.0 / z

        cap = []
        for j in range(K):
            cnt = (jnp.where(cand[j] >= thr, 1.0, 0.0)
                   + jnp.where(a_hi + b[j] >= thr, 1.0, 0.0))
            for shift in (4, 2, 1):
                cnt = cnt + pltpu.roll(cnt, shift, 0)
            cap.append(cnt)

        r1_rows, e1_rows, p2_rows, e2_rows = [], [], [], []
        for i in range(ngroups):
            s1 = ss_ref[2 * h, SUBLANES * i:SUBLANES * (i + 1), :]
            s2 = ss_ref[2 * h + 1, SUBLANES * i:SUBLANES * (i + 1), :]
            rank1 = jnp.full_like(s1, K + 1.0)
            cap2 = jnp.zeros_like(s2)
            for r in reversed(range(K)):
                rank1 = jnp.where(s1 == a[r], r + 1.0, rank1)
                cap2 = jnp.where(s2 == b[r], cap[r], cap2)
            r1_rows.append(rank1)
            p2_rows.append(cap2)
            e1_rows.append(jnp.exp(s1 - a[0]))
            e2_rows.append(jnp.exp(s2 - b[0]) * (0.5 * inv_z))
        r1_ref[h] = _dup_bf16_bits(jnp.concatenate(r1_rows, axis=0))
        e1_ref[h] = _dup_bf16_bits(jnp.concatenate(e1_rows, axis=0))
        p2_ref[h] = pltpu.bitcast(jnp.concatenate(p2_rows, axis=0).astype(BF16), jnp.uint32)
        e2_ref[h] = pltpu.bitcast(jnp.concatenate(e2_rows, axis=0).astype(BF16), jnp.uint32)
        return carry

    lax.fori_loop(0, PEER_HEADS, head_body, 0)


def _route(ss, tf):
    nhp, nkeys, N = ss.shape
    spec_in = pl.BlockSpec((nhp, nkeys, tf), lambda t: (0, 0, t))
    spec_out = pl.BlockSpec((PEER_HEADS, nkeys, tf), lambda t: (0, 0, t))
    spec_packed = pl.BlockSpec((PEER_HEADS, nkeys // 2, tf), lambda t: (0, 0, t))
    f32s = jax.ShapeDtypeStruct((PEER_HEADS, nkeys, N), jnp.uint32)
    bf16s = jax.ShapeDtypeStruct((PEER_HEADS, nkeys // 2, N), jnp.uint32)
    return pl.pallas_call(
        _route_kernel,
        grid=(N // tf,),
        in_specs=[spec_in],
        out_specs=[spec_out, spec_out, spec_packed, spec_packed],
        out_shape=[f32s, f32s, bf16s, bf16s],
        compiler_params=_params("arbitrary"),
        name="route",
    )(ss)


def _peer_kernel(h2t_ref, u_ref, vt_ref, r1_ref, e1_ref, p2_ref, e2_ref, x1_ref, mod_ref,
                 gpost_ref, o_ref, acc_ref, xa_ref, xb_ref, wa_ref, wb_ref):
    e = pl.program_id(2)
    n_e = pl.num_programs(2)
    n_sub = u_ref.shape[0]
    tg = h2t_ref.shape[1]
    rows_per_sub = PEER_SUB // PEER_NKEYS
    chunk = 2 * SUBLANES
    n_chunks = PEER_NKEYS // chunk
    c0 = math.sqrt(2.0 / math.pi)

    @pl.when(e == 0)
    def _():
        acc_ref[...] = jnp.zeros_like(acc_ref)

    xbufs = (xa_ref, xb_ref)
    wbufs = (wa_ref, wb_ref)

    tw = xa_ref.shape[1]
    n_split = tg // tw

    def unit(k):
        return k // n_split, slice((k % n_split) * tw, (k % n_split + 1) * tw)

    def scores_into(k):
        sb, cols = unit(k)
        xbufs[k % 2][...] = _dot(u_ref[sb], h2t_ref[:, cols]).astype(BF16)

    def stack(parts):
        return jnp.concatenate(parts, axis=0)

    def tile(ref, h, c, lanes):
        return pltpu.bitcast(ref[h, c * SUBLANES:(c + 1) * SUBLANES, lanes], BF16)

    def gates(k):
        sb, cols = unit(k)
        xbuf_ref = xbufs[k % 2]
        wbuf_ref = wbufs[k % 2]
        i1 = sb * rows_per_sub

        def row_bcast(ref, h, lanes):
            return stack([pltpu.bitcast(
                jnp.broadcast_to(ref[h, i1 + j:i1 + j + 1, lanes], (SUBLANES, 128)), BF16)
                for j in range(rows_per_sub)])

        zero = jnp.zeros((rows_per_sub * chunk, 128), BF16)
        for lg in range(tw // 128):
            lanes = slice(cols.start + lg * 128, cols.start + (lg + 1) * 128)
            local = slice(lg * 128, (lg + 1) * 128)
            g = [zero] * n_chunks
            for h in range(PEER_HEADS):
                rank1 = row_bcast(r1_ref, h, lanes)
                gate1 = row_bcast(e1_ref, h, lanes)
                for c in range(n_chunks):
                    cap2 = stack([tile(p2_ref, h, c, lanes)] * rows_per_sub)
                    gate2 = stack([tile(e2_ref, h, c, lanes)] * rows_per_sub)
                    g[c] = g[c] + jnp.where(rank1 <= cap2, gate2, 0.0) * gate1
            for c in range(n_chunks):
                x = stack([xbuf_ref[j * PEER_NKEYS + c * chunk:j * PEER_NKEYS + (c + 1) * chunk, local]
                           for j in range(rows_per_sub)])
                w = x * (1.0 + jnp.tanh(x * (c0 + (c0 * 0.044715) * (x * x)))) * g[c]
                for j in range(rows_per_sub):
                    wbuf_ref[j * PEER_NKEYS + c * chunk:j * PEER_NKEYS + (c + 1) * chunk, local] = (
                        w[j * chunk:(j + 1) * chunk])

    def accumulate(k):
        sb, cols = unit(k)
        acc_ref[:, cols] += _dot(vt_ref[sb], wbufs[k % 2][...])

    n_units = n_sub * n_split
    scores_into(0)
    for k in range(n_units):
        if k + 1 < n_units:
            scores_into(k + 1)
        if k >= 1:
            accumulate(k - 1)
        gates(k)
    accumulate(n_units - 1)

    @pl.when(e == n_e - 1)
    def _():
        y = acc_ref[...].T
        gate2 = mod_ref[0, 5:6, :]
        o_ref[...] = x1_ref[...] + gate2 * _rms(y, gpost_ref[...])


def _peer(h2t, u3, vt3, r1, e1, p2, e2, x1, mod3, g_post, B, S, tg, n_sub):
    D, N = h2t.shape
    n_sub_total = u3.shape[0]
    nblk = S // tg
    tok = lambda b, s, e: (b * nblk + s, 0)
    tab_spec = pl.BlockSpec((PEER_HEADS, PEER_NKEYS // 2, tg), lambda b, s, e: (0, 0, b * nblk + s))
    rows_per_step = n_sub * PEER_SUB // PEER_NKEYS
    row_spec = pl.BlockSpec((PEER_HEADS, rows_per_step, tg), lambda b, s, e: (0, e, b * nblk + s))
    return pl.pallas_call(
        _peer_kernel,
        grid=(B, nblk, n_sub_total // n_sub),
        in_specs=[pl.BlockSpec((D, tg), lambda b, s, e: (0, b * nblk + s)),
                  pl.BlockSpec((n_sub, PEER_SUB, D), lambda b, s, e: (e, 0, 0)),
                  pl.BlockSpec((n_sub, D, PEER_SUB), lambda b, s, e: (e, 0, 0)),
                  row_spec, row_spec, tab_spec, tab_spec,
                  pl.BlockSpec((tg, D), tok),
                  pl.BlockSpec((1, 6, D), lambda b, s, e: (b, 0, 0)),
                  pl.BlockSpec((1, D), lambda b, s, e: (0, 0))],
        out_specs=pl.BlockSpec((tg, D), tok),
        out_shape=jax.ShapeDtypeStruct((N, D), F32),
        scratch_shapes=[pltpu.VMEM((D, tg), F32),
                        pltpu.VMEM((PEER_SUB, PEER_UNIT_TOKENS), BF16),
                        pltpu.VMEM((PEER_SUB, PEER_UNIT_TOKENS), BF16),
                        pltpu.VMEM((PEER_SUB, PEER_UNIT_TOKENS), BF16),
                        pltpu.VMEM((PEER_SUB, PEER_UNIT_TOKENS), BF16)],
        compiler_params=_params("arbitrary", "arbitrary", "arbitrary"),
        name="peer",
    )(h2t, u3, vt3, r1, e1, p2, e2, x1, mod3, g_post)


def _block(n, cap):
    b = min(n, cap)
    assert n % b == 0, (n, b)
    return b


def _layer(x, c, w_ada, b_ada, norm_pre_mix, norm_post_mix, w_in, w_alpha, b_alpha, gla_norm,
           conv_w, conv_b, lru_w_a, lru_b_a, lru_w_x, lru_b_x, lru_lambda, w_out,
           norm_pre_ffn, norm_post_ffn, peer_w_query, peer_sub_keys, peer_u, peer_v):
    B, S, D = x.shape
    N = B * S
    dk = w_alpha.shape[1]
    rank = w_alpha.shape[0]
    assert S % GLA_CHUNK == 0 and D % 128 == 0

    o_q, o_k, o_v, o_r, o_al, o_xl, o_xg, o_ga, o_gb = (
        0, dk, 2 * dk, 2 * dk + D, 2 * dk + 2 * D, 2 * dk + 2 * D + rank,
        2 * dk + 3 * D + rank, 2 * dk + 4 * D + rank, 2 * dk + 5 * D + rank)
    w_main = jnp.concatenate(
        [w_in[:, o_q:o_r], w_in[:, o_r:o_al], w_in[:, o_xl:]], axis=1).astype(BF16)
    w_al = jnp.pad(w_in[:, o_al:o_xl], ((0, 0), (0, GLA_RANK_PAD - rank))).astype(BF16)
    w_alpha_p = jnp.pad(w_alpha, ((0, GLA_RANK_PAD - rank), (0, 0))).astype(BF16)
    row = lambda a: a.reshape(1, -1).astype(F32)

    mod3 = _ada(c.astype(F32), w_ada.astype(F32), b_ada.astype(F32)).reshape(B, 6, D)
    x2 = x.reshape(N, D)

    tb = _block(S, 512)
    qk, v, r, xl, xg, ga, gb, la = _inproj(
        x2, mod3, row(norm_pre_mix), w_main, w_al, w_alpha_p, row(b_alpha), B, S, tb)

    y_gla = _gla(qk, v, r, la, row(gla_norm), B, S, _block(S, 256))
    y_lru = _lru(xl, xg, conv_w.astype(F32), row(conv_b), lru_w_a.astype(BF16), row(lru_b_a),
                 lru_w_x.astype(BF16), row(lru_b_x), row(lru_lambda), B, S, _block(S, 32))

    nh, _, nkeys, half = peer_sub_keys.shape
    keys = peer_sub_keys.reshape(nh * 2, nkeys, half).astype(BF16)
    x1, h2t, ss = _mix(y_gla, y_lru, ga, gb, x2, mod3, w_out.astype(BF16), row(norm_post_mix),
                      row(norm_pre_ffn), peer_w_query.astype(BF16), keys, B, S, _block(S, 512))

    r1, e1, p2, e2 = _route(ss, _block(N, 256))

    n_exp = peer_u.shape[0]
    u3 = peer_u.astype(BF16).reshape(n_exp // PEER_SUB, PEER_SUB, D)
    vt3 = peer_v.astype(BF16).reshape(n_exp // PEER_SUB, PEER_SUB, D).transpose(0, 2, 1)
    out = _peer(h2t, u3, vt3, r1, e1, p2, e2, x1, mod3, row(norm_post_ffn), B, S,
                _block(S, 512), 8)
    return out.reshape(B, S, D)


def kernel(x, c, w_ada, b_ada, norm_pre_mix, norm_post_mix, w_in, w_alpha, b_alpha, gla_norm, conv_w, conv_b, lru_w_a, lru_b_a, lru_w_x, lru_b_x, lru_lambda, w_out, norm_pre_ffn, norm_post_ffn, peer_w_query, peer_sub_keys, peer_u, peer_v):
    depth = w_ada.shape[0]
    for l in range(depth):
        x = _layer(x, c, w_ada[l], b_ada[l], norm_pre_mix[l], norm_post_mix[l], w_in[l],
                   w_alpha[l], b_alpha[l], gla_norm[l], conv_w[l], conv_b[l], lru_w_a[l],
                   lru_b_a[l], lru_w_x[l], lru_b_x[l], lru_lambda[l], w_out[l],
                   norm_pre_ffn[l], norm_post_ffn[l], peer_w_query[l], peer_sub_keys[l],
                   peer_u[l], peer_v[l])
    return x
```

```python
import functools
import math

import jax
import jax.numpy as jnp
from jax import lax
from jax.experimental import pallas as pl
from jax.experimental.pallas import tpu as pltpu

F32 = jnp.float32
BF16 = jnp.bfloat16

EPS = 1e-6
GLA_HEADS = 4
GLA_RANK_PAD = 128
GLA_TAU = 16.0
GLA_CHUNK = 128
LRU_BLOCKS = 8
CONV_WIDTH = 4
LRU_C = 8.0
PEER_HEADS = 8
PEER_NKEYS = 128
PEER_TOPK = 16
PEER_SUB = 256
PEER_UNIT_TOKENS = 256
PEER_ROUTE_TOKENS = 256
PEER_TABLE_PAD = 128
SUBLANES = 8

VMEM_LIMIT = 56 * 1024 * 1024

NT_DIMS = (((1,), (1,)), ((), ()))
TN_DIMS = (((0,), (0,)), ((), ()))


def _dot(a, b):
    return jnp.dot(a, b, preferred_element_type=F32)


def _dot_nt(a, b):
    return lax.dot_general(a, b, NT_DIMS, preferred_element_type=F32)


def _dot_tn(a, b):
    return lax.dot_general(a, b, TN_DIMS, preferred_element_type=F32)


def _sigmoid(x):
    return 1.0 / (1.0 + jnp.exp(-x))


def _gelu_tanh(x):
    c0 = math.sqrt(2.0 / math.pi)
    return 0.5 * x * (1.0 + jnp.tanh(c0 * (x + 0.044715 * (x * x * x))))


def _rms(x, gain):
    ms = jnp.mean(x * x, axis=-1, keepdims=True)
    return x * lax.rsqrt(ms + EPS) * gain


def _params(*sem, flags=None):
    return pltpu.CompilerParams(dimension_semantics=sem, vmem_limit_bytes=VMEM_LIMIT, flags=flags)


def _ada_kernel(c_ref, w_ref, b_ref, o_ref):
    c = c_ref[...]
    s = c * _sigmoid(c)
    o_ref[...] = jnp.dot(s, w_ref[...], precision=lax.Precision.HIGHEST,
                         preferred_element_type=F32) + b_ref[...]


def _ada(c, w_ada, b_ada):
    B, D = c.shape
    n_out = w_ada.shape[1]
    blk = 1024
    return pl.pallas_call(
        _ada_kernel,
        grid=(n_out // blk,),
        in_specs=[pl.BlockSpec((B, D), lambda j: (0, 0)),
                  pl.BlockSpec((D, blk), lambda j: (0, j)),
                  pl.BlockSpec((1, blk), lambda j: (0, j))],
        out_specs=pl.BlockSpec((B, blk), lambda j: (0, j)),
        out_shape=jax.ShapeDtypeStruct((B, n_out), F32),
        compiler_params=_params("arbitrary"),
        name="ada",
    )(c, w_ada, b_ada.reshape(1, n_out))


def _inproj_kernel(x_ref, mod_ref, gain_ref, w_ref, wal_ref, walpha_ref, balpha_ref,
                   qk_ref, v_ref, r_ref, xl_ref, xg_ref, ga_ref, gb_ref, la_ref):
    D = x_ref.shape[-1]
    x = x_ref[...]
    shift = mod_ref[0, 0:1, :]
    scale = mod_ref[0, 1:2, :]
    h = (_rms(x, gain_ref[...]) * (1.0 + scale) + shift).astype(BF16)
    for j, o_ref in enumerate((qk_ref, v_ref, r_ref, xl_ref, xg_ref, ga_ref, gb_ref)):
        o_ref[...] = _dot(h, w_ref[:, j * D:(j + 1) * D]).astype(BF16)
    a_low = _dot(h, wal_ref[...]).astype(BF16)
    z = _dot(a_low, walpha_ref[...]) + balpha_ref[...]
    log_sig = jnp.minimum(z, 0.0) - jnp.log(1.0 + jnp.exp(-jnp.abs(z)))
    la_ref[...] = log_sig * (1.0 / GLA_TAU)


def _inproj(x2, mod3, gain, w_main, w_al, w_alpha, b_alpha, B, S, tb):
    N, D = x2.shape
    dk = w_alpha.shape[1]
    nblk = S // tb
    tok = lambda b, s: (b * nblk + s, 0)
    const = lambda b, s: (0, 0)
    big = jax.ShapeDtypeStruct((N, D), BF16)
    return pl.pallas_call(
        _inproj_kernel,
        grid=(B, nblk),
        in_specs=[pl.BlockSpec((tb, D), tok),
                  pl.BlockSpec((1, 6, D), lambda b, s: (b, 0, 0)),
                  pl.BlockSpec((1, D), const),
                  pl.BlockSpec(w_main.shape, const, pipeline_mode=pl.Buffered(1)),
                  pl.BlockSpec(w_al.shape, const),
                  pl.BlockSpec(w_alpha.shape, const),
                  pl.BlockSpec((1, dk), const)],
        out_specs=[pl.BlockSpec((tb, D), tok)] * 7 + [pl.BlockSpec((tb, dk), tok)],
        out_shape=[big] * 7 + [jax.ShapeDtypeStruct((N, dk), F32)],
        compiler_params=_params("arbitrary", "arbitrary"),
        name="inproj",
    )(x2, mod3, gain, w_main, w_al, w_alpha, b_alpha)


def _gla_kernel(qk_ref, v_ref, r_ref, la_ref, gn_ref, tri_ref, o_ref, st_ref):
    tc = qk_ref.shape[0]
    dk_all = la_ref.shape[-1]
    hdk = dk_all // GLA_HEADS
    hdv = v_ref.shape[-1] // GLA_HEADS
    C = GLA_CHUNK
    n_c = tc // C
    mid = C // 2 - 1

    @pl.when(pl.program_id(1) == 0)
    def _():
        st_ref[...] = jnp.zeros_like(st_ref)

    la = la_ref[...]
    p0 = la.astype(BF16)
    rem = la - p0.astype(F32)
    p1 = rem.astype(BF16)
    p2 = (rem - p1.astype(F32)).astype(BF16)
    tri = tri_ref[...]
    cum = _dot(tri, p0) + _dot(tri, p1) + _dot(tri, p2)

    def per_chunk_row(r):
        return jnp.concatenate(
            [jnp.broadcast_to(cum[c * C + r:c * C + r + 1], (C, dk_all)) for c in range(n_c)], axis=0)

    cm = per_chunk_row(mid)
    cl = per_chunk_row(C - 1)
    qt = qk_ref[:, 0:dk_all].astype(F32) * (hdk ** -0.5) * jnp.exp(cum - cm)
    kt = qk_ref[:, dk_all:2 * dk_all].astype(F32) * jnp.exp(cm - cum)
    qi = (qt * jnp.exp(cm)).astype(BF16)
    kd = (kt * jnp.exp(cl - cm)).astype(BF16)
    qt = qt.astype(BF16)
    kt = kt.astype(BF16)

    row = lax.broadcasted_iota(jnp.int32, (C, C), 0)
    col = lax.broadcasted_iota(jnp.int32, (C, C), 1)
    causal = row >= col
    pairs = [(c, h) for c in range(n_c) for h in range(GLA_HEADS)]
    rows = lambda c: slice(c * C, (c + 1) * C)
    ks = lambda h: slice(h * hdk, (h + 1) * hdk)
    vs = lambda h: slice(h * hdv, (h + 1) * hdv)

    scores = {p: jnp.where(causal, _dot_nt(qt[rows(p[0]), ks(p[1])], kt[rows(p[0]), ks(p[1])]),
                           0.0).astype(BF16) for p in pairs}
    intra = {p: _dot(scores[p], v_ref[rows(p[0]), vs(p[1])]) for p in pairs}
    update = {p: _dot_tn(v_ref[rows(p[0]), vs(p[1])], kd[rows(p[0]), ks(p[1])]) for p in pairs}
    state_in = {}
    for h in range(GLA_HEADS):
        st = st_ref[h]
        for c in range(n_c):
            state_in[(c, h)] = st.astype(BF16)
            e_last = jnp.exp(cum[(c + 1) * C - 1:(c + 1) * C, ks(h)])
            st = st * e_last + update[(c, h)]
        st_ref[h] = st
    for c, h in pairs:
        o = intra[(c, h)] + _dot_nt(qi[rows(c), ks(h)], state_in[(c, h)])
        y = _rms(o, gn_ref[:, vs(h)])
        rr = r_ref[rows(c), vs(h)].astype(F32)
        o_ref[rows(c), vs(h)] = (y * (rr * _sigmoid(rr))).astype(BF16)


def _gla(qk, v, r, la, gla_norm, B, S, tc):
    N, D = v.shape
    dk = la.shape[1]
    nblk = S // tc
    tok = lambda b, s: (b * nblk + s, 0)
    const = lambda b, s: (0, 0)
    idx = jnp.arange(tc)
    tri = ((idx[:, None] >= idx[None, :]) &
           (idx[:, None] // GLA_CHUNK == idx[None, :] // GLA_CHUNK)).astype(BF16)
    return pl.pallas_call(
        _gla_kernel,
        grid=(B, nblk),
        in_specs=[pl.BlockSpec((tc, 2 * dk), tok),
                  pl.BlockSpec((tc, D), tok),
                  pl.BlockSpec((tc, D), tok),
                  pl.BlockSpec((tc, dk), tok),
                  pl.BlockSpec((1, D), const),
                  pl.BlockSpec((tc, tc), const)],
        out_specs=pl.BlockSpec((tc, D), tok),
        out_shape=jax.ShapeDtypeStruct((N, D), BF16),
        scratch_shapes=[pltpu.VMEM((GLA_HEADS, D // GLA_HEADS, dk // GLA_HEADS), F32)],
        compiler_params=_params("arbitrary", "arbitrary"),
        name="gla",
    )(qk, v, r, la, gla_norm, tri)


def _lru_kernel(xl_ref, xg_ref, perm_ref, cw_ref, cb_ref, wa_ref, ba_ref, wx_ref, bx_ref,
                lam_ref, o_ref, xbuf_ref, h_ref):
    nb, tt, W = xl_ref.shape
    rows = nb * tt
    bw = W // LRU_BLOCKS
    tail = (CONV_WIDTH - 1) * nb

    @pl.when(pl.program_id(0) == 0)
    def _():
        xbuf_ref[0:tail, :] = jnp.zeros((tail, W), F32)
        h_ref[...] = jnp.zeros_like(h_ref)

    perm = perm_ref[...]
    xl = _dot(perm, xl_ref[...].reshape(rows, W))
    xg = _dot(perm, xg_ref[...].reshape(rows, W))

    xbuf_ref[tail:tail + rows, :] = xl
    xc = cb_ref[...]
    for j in range(CONV_WIDTH):
        xc = xc + cw_ref[j:j + 1, :] * xbuf_ref[j * nb:j * nb + rows, :]
    xbuf_ref[0:tail, :] = xbuf_ref[rows:rows + tail, :]

    xcb = xc.astype(BF16)
    gr_parts, gi_parts = [], []
    for n in range(LRU_BLOCKS):
        blk = xcb[:, n * bw:(n + 1) * bw]
        gr_parts.append(_dot(blk, wa_ref[n]))
        gi_parts.append(_dot(blk, wx_ref[n]))
    gate_r = _sigmoid(jnp.concatenate(gr_parts, axis=1) + ba_ref[...])
    gate_i = _sigmoid(jnp.concatenate(gi_parts, axis=1) + bx_ref[...])
    neg_lam = -lam_ref[...]
    softplus = jnp.maximum(neg_lam, 0.0) + jnp.log(1.0 + jnp.exp(-jnp.abs(neg_lam)))
    log_a = (-LRU_C) * gate_r * softplus
    a = jnp.exp(log_a)
    u = jnp.sqrt(jnp.maximum(1.0 - jnp.exp(2.0 * log_a), 0.0)) * (gate_i * xc)

    h = h_ref[...]
    steps = []
    for t in range(tt):
        grp = slice(t * nb, (t + 1) * nb)
        h = a[grp] * h + u[grp]
        steps.append(h)
    h_ref[...] = h
    y = (jnp.concatenate(steps, axis=0) * _gelu_tanh(xg)).astype(BF16)
    y = lax.dot_general(perm, y, TN_DIMS, preferred_element_type=F32).astype(BF16)
    o_ref[...] = y.reshape(nb, tt, W)


def _lru(xl, xg, conv_w, conv_b, wa, ba, wx, bx, lam, B, S, tt):
    N, W = xl.shape
    assert B == SUBLANES, "the time-major layout places the batch on the sublane axis"
    rows = B * tt
    r = jnp.arange(rows)
    perm = (r[None, :] == (r[:, None] % B) * tt + r[:, None] // B).astype(BF16)
    blk = lambda s: (0, s, 0)
    const2 = lambda s: (0, 0)
    const3 = lambda s: (0, 0, 0)
    out = pl.pallas_call(
        _lru_kernel,
        grid=(S // tt,),
        in_specs=[pl.BlockSpec((B, tt, W), blk),
                  pl.BlockSpec((B, tt, W), blk),
                  pl.BlockSpec((rows, rows), const2),
                  pl.BlockSpec(conv_w.shape, const2),
                  pl.BlockSpec((1, W), const2),
                  pl.BlockSpec(wa.shape, const3),
                  pl.BlockSpec((1, W), const2),
                  pl.BlockSpec(wx.shape, const3),
                  pl.BlockSpec((1, W), const2),
                  pl.BlockSpec((1, W), const2)],
        out_specs=pl.BlockSpec((B, tt, W), blk),
        out_shape=jax.ShapeDtypeStruct((B, S, W), BF16),
        scratch_shapes=[pltpu.VMEM((rows + (CONV_WIDTH - 1) * B, W), F32),
                        pltpu.VMEM((B, W), F32)],
        compiler_params=_params("arbitrary"),
        name="lru",
    )(xl.reshape(B, S, W), xg.reshape(B, S, W), perm, conv_w, conv_b, wa, ba, wx, bx, lam)
    return out.reshape(N, W)


def _mix_kernel(yg_ref, yl_ref, ga_ref, gb_ref, x_ref, mod_ref, wout_ref, gpost_ref,
                gpre_ref, wq_ref, keys_ref, x1_ref, h2t_ref, ss_ref):
    merged = (_sigmoid(ga_ref[...].astype(F32)) * yg_ref[...].astype(F32)
              + _sigmoid(gb_ref[...].astype(F32)) * yl_ref[...].astype(F32))
    y = _dot(merged.astype(BF16), wout_ref[...])
    gate1 = mod_ref[0, 2:3, :]
    x1 = x_ref[...] + gate1 * _rms(y, gpost_ref[...])
    x1_ref[...] = x1
    shift2 = mod_ref[0, 3:4, :]
    scale2 = mod_ref[0, 4:5, :]
    h2f = _rms(x1, gpre_ref[...]) * (1.0 + scale2) + shift2
    h2 = h2f.astype(BF16)
    h2t_ref[...] = h2f.T.astype(BF16)
    qry = _dot(h2, wq_ref[...]).astype(BF16)
    half = keys_ref.shape[-1]
    for hp in range(keys_ref.shape[0]):
        ss_ref[hp] = _dot_nt(keys_ref[hp], qry[:, hp * half:(hp + 1) * half])


def _mix(yg, yl, ga, gb, x2, mod3, w_out, g_post, g_pre, w_q, keys, B, S, te):
    N, D = x2.shape
    nblk = S // te
    tok = lambda b, s: (b * nblk + s, 0)
    const2 = lambda b, s: (0, 0)
    nhp, nkeys, _ = keys.shape
    return pl.pallas_call(
        _mix_kernel,
        grid=(B, nblk),
        in_specs=[pl.BlockSpec((te, D), tok)] * 5 + [
            pl.BlockSpec((1, 6, D), lambda b, s: (b, 0, 0)),
            pl.BlockSpec(w_out.shape, const2),
            pl.BlockSpec((1, D), const2),
            pl.BlockSpec((1, D), const2),
            pl.BlockSpec(w_q.shape, const2),
            pl.BlockSpec(keys.shape, lambda b, s: (0, 0, 0))],
        out_specs=[pl.BlockSpec((te, D), tok),
                   pl.BlockSpec((D, te), lambda b, s: (0, b * nblk + s)),
                   pl.BlockSpec((nhp, nkeys, te), lambda b, s: (0, 0, b * nblk + s))],
        out_shape=[jax.ShapeDtypeStruct((N, D), F32),
                   jax.ShapeDtypeStruct((D, N), BF16),
                   jax.ShapeDtypeStruct((nhp, nkeys, N), F32)],
        compiler_params=_params("arbitrary", "arbitrary"),
        name="mix",
    )(yg, yl, ga, gb, x2, mod3, w_out, g_post, g_pre, w_q, keys)


def _oddeven_merge_sort_pairs(n):
    pairs = []
    p = 1
    while p < n:
        k = p
        while k >= 1:
            for j in range(k % p, n - k, 2 * k):
                for i in range(min(k, n - j - k)):
                    if (i + j) // (2 * p) == (i + j + k) // (2 * p):
                        pairs.append((i + j, i + j + k))
            k //= 2
        p *= 2
    return pairs


_SORT16 = _oddeven_merge_sort_pairs(PEER_TOPK)


def _sort_desc(vals):
    vals = list(vals)
    for i, j in _SORT16:
        hi = jnp.maximum(vals[i], vals[j])
        lo = jnp.minimum(vals[i], vals[j])
        vals[i], vals[j] = hi, lo
    return vals


def _bitonic_merge_desc(vals):
    vals = list(vals)
    n = len(vals)
    d = n // 2
    while d >= 1:
        for i in range(n):
            if (i & d) == 0:
                hi = jnp.maximum(vals[i], vals[i + d])
                lo = jnp.minimum(vals[i], vals[i + d])
                vals[i], vals[i + d] = hi, lo
        d //= 2
    return vals


def _merge_across_sublanes(vals):
    n = len(vals)
    for shift in (4, 2, 1):
        partner = [pltpu.roll(v, shift, 0) for v in vals]
        vals = _bitonic_merge_desc([jnp.maximum(vals[i], partner[n - 1 - i]) for i in range(n)])
    return vals


def _top16_sorted(s_ref, hp, lanes):
    groups = [s_ref[hp, SUBLANES * i:SUBLANES * (i + 1), lanes]
              for i in range(PEER_NKEYS // SUBLANES)]
    return _merge_across_sublanes(_sort_desc(groups))


def _dup_bf16_bits(x):
    hi = pltpu.bitcast(x.astype(BF16).astype(F32), jnp.uint32)
    return hi | lax.shift_right_logical(hi, jnp.full(hi.shape, 16, jnp.uint32))


def _route_kernel(ss_ref, r1_ref, e1_ref, p2_ref, e2_ref):
    tokens = ss_ref.shape[-1]
    tf = PEER_ROUTE_TOKENS
    K = PEER_TOPK
    ngroups = PEER_NKEYS // SUBLANES
    sub = lax.broadcasted_iota(jnp.int32, (SUBLANES, tf), 0)

    def route_lanes(h, lanes):
        a = _top16_sorted(ss_ref, 2 * h, lanes)
        b = _top16_sorted(ss_ref, 2 * h + 1, lanes)
        a_lo = a[0]
        a_hi = a[SUBLANES]
        for i in range(1, SUBLANES):
            a_lo = jnp.where(sub == i, a[i], a_lo)
            a_hi = jnp.where(sub == i, a[SUBLANES + i], a_hi)
        cand = [a_lo + b[j] for j in range(K)]
        extra = a_hi + b[0]
        ins = [jnp.maximum(cand[0], extra)]
        for j in range(1, K):
            ins.append(jnp.maximum(cand[j], jnp.minimum(cand[j - 1], extra)))
        top = _merge_across_sublanes(ins)
        thr = top[K - 1]
        smax = top[0]
        z = jnp.exp(top[0] - smax)
        for j in range(1, K):
            z = z + jnp.exp(top[j] - smax)
        inv_z = 1.0 / z

        cap = []
        for j in range(K):
            cnt = (jnp.where(cand[j] >= thr, 1.0, 0.0)
                   + jnp.where(a_hi + b[j] >= thr, 1.0, 0.0))
            for shift in (4, 2, 1):
                cnt = cnt + pltpu.roll(cnt, shift, 0)
            cap.append(cnt)

        r1_rows, e1_rows, p2_rows, e2_rows = [], [], [], []
        for i in range(ngroups):
            s1 = ss_ref[2 * h, SUBLANES * i:SUBLANES * (i + 1), lanes]
            s2 = ss_ref[2 * h + 1, SUBLANES * i:SUBLANES * (i + 1), lanes]
            rank1 = jnp.full_like(s1, K + 1.0)
            cap2 = jnp.zeros_like(s2)
            for r in reversed(range(K)):
                rank1 = jnp.where(s1 == a[r], r + 1.0, rank1)
                cap2 = jnp.where(s2 == b[r], cap[r], cap2)
            r1_rows.append(rank1)
            p2_rows.append(cap2)
            e1_rows.append(jnp.exp(s1 - a[0]))
            e2_rows.append(jnp.exp(s2 - b[0]) * (0.5 * inv_z))
        r1_ref[0, h, :, lanes] = _dup_bf16_bits(jnp.concatenate(r1_rows, axis=0))
        e1_ref[0, h, :, lanes] = _dup_bf16_bits(jnp.concatenate(e1_rows, axis=0))
        p2_ref[0, h, :, lanes] = pltpu.bitcast(
            jnp.concatenate(p2_rows, axis=0).astype(BF16), jnp.uint32)
        e2_ref[0, h, :, lanes] = pltpu.bitcast(
            jnp.concatenate(e2_rows, axis=0).astype(BF16), jnp.uint32)

    def head_body(h, carry):
        for part in range(tokens // tf):
            route_lanes(h, slice(part * tf, (part + 1) * tf))
        for ref in (r1_ref, e1_ref, p2_ref, e2_ref):
            ref[0, h, :, tokens:] = jnp.zeros((ref.shape[2], ref.shape[3] - tokens), jnp.uint32)
        return carry

    lax.fori_loop(0, PEER_HEADS, head_body, 0)


def _route(ss, tg):
    nhp, nkeys, N = ss.shape
    pitch = tg + PEER_TABLE_PAD
    spec_in = pl.BlockSpec((nhp, nkeys, tg), lambda t: (0, 0, t))
    spec_out = pl.BlockSpec((1, PEER_HEADS, nkeys, pitch), lambda t: (t, 0, 0, 0))
    spec_packed = pl.BlockSpec((1, PEER_HEADS, nkeys // 2, pitch), lambda t: (t, 0, 0, 0))
    f32s = jax.ShapeDtypeStruct((N // tg, PEER_HEADS, nkeys, pitch), jnp.uint32)
    bf16s = jax.ShapeDtypeStruct((N // tg, PEER_HEADS, nkeys // 2, pitch), jnp.uint32)
    return pl.pallas_call(
        _route_kernel,
        grid=(N // tg,),
        in_specs=[spec_in],
        out_specs=[spec_out, spec_out, spec_packed, spec_packed],
        out_shape=[f32s, f32s, bf16s, bf16s],
        compiler_params=_params("arbitrary"),
        name="route",
    )(ss)


def _peer_kernel(h2t_ref, u_ref, vt_ref, r1_ref, e1_ref, p2_ref, e2_ref, x1_ref, mod_ref,
                 gpost_ref, o_ref, acc_ref, xa_ref, xb_ref, wa_ref, wb_ref):
    e = pl.program_id(2)
    n_e = pl.num_programs(2)
    n_sub = u_ref.shape[0]
    tg = h2t_ref.shape[1]
    rows_per_sub = PEER_SUB // PEER_NKEYS
    chunk = 2 * SUBLANES
    n_chunks = PEER_NKEYS // chunk
    c0 = math.sqrt(2.0 / math.pi)

    @pl.when(e == 0)
    def _():
        acc_ref[...] = jnp.zeros_like(acc_ref)

    xbufs = (xa_ref, xb_ref)
    wbufs = (wa_ref, wb_ref)

    tw = xa_ref.shape[1]
    n_split = tg // tw

    def unit(k):
        return k // n_split, slice((k % n_split) * tw, (k % n_split + 1) * tw)

    def scores_into(k):
        sb, cols = unit(k)
        xbufs[k % 2][...] = _dot(u_ref[sb], h2t_ref[:, cols]).astype(BF16)

    def stack(parts):
        return jnp.concatenate(parts, axis=0)

    def tile(ref, h, c, lanes):
        return pltpu.bitcast(ref[0, h, c * SUBLANES:(c + 1) * SUBLANES, lanes], BF16)

    def gates(k):
        sb, cols = unit(k)
        xbuf_ref = xbufs[k % 2]
        wbuf_ref = wbufs[k % 2]
        i1 = sb * rows_per_sub

        def row_bcast(ref, h, lanes):
            return stack([pltpu.bitcast(
                jnp.broadcast_to(ref[0, h, i1 + j:i1 + j + 1, lanes], (SUBLANES, 128)), BF16)
                for j in range(rows_per_sub)])

        zero = jnp.zeros((rows_per_sub * chunk, 128), BF16)
        for lg in range(tw // 128):
            lanes = slice(cols.start + lg * 128, cols.start + (lg + 1) * 128)
            local = slice(lg * 128, (lg + 1) * 128)
            g = [zero] * n_chunks
            for h in range(PEER_HEADS):
                rank1 = row_bcast(r1_ref, h, lanes)
                gate1 = row_bcast(e1_ref, h, lanes)
                for c in range(n_chunks):
                    cap2 = stack([tile(p2_ref, h, c, lanes)] * rows_per_sub)
                    gate2 = stack([tile(e2_ref, h, c, lanes)] * rows_per_sub)
                    g[c] = g[c] + jnp.where(rank1 <= cap2, gate2, 0.0) * gate1
            for c in range(n_chunks):
                x = stack([xbuf_ref[j * PEER_NKEYS + c * chunk:j * PEER_NKEYS + (c + 1) * chunk, local]
                           for j in range(rows_per_sub)])
                w = x * (1.0 + jnp.tanh(x * (c0 + (c0 * 0.044715) * (x * x)))) * g[c]
                for j in range(rows_per_sub):
                    wbuf_ref[j * PEER_NKEYS + c * chunk:j * PEER_NKEYS + (c + 1) * chunk, local] = (
                        w[j * chunk:(j + 1) * chunk])

    def accumulate(k):
        sb, cols = unit(k)
        acc_ref[:, cols] += _dot(vt_ref[sb], wbufs[k % 2][...])

    n_units = n_sub * n_split
    scores_into(0)
    for k in range(n_units):
        if k + 1 < n_units:
            scores_into(k + 1)
        if k >= 1:
            accumulate(k - 1)
        gates(k)
    accumulate(n_units - 1)

    @pl.when(e == n_e - 1)
    def _():
        y = acc_ref[...].T
        gate2 = mod_ref[0, 5:6, :]
        o_ref[...] = x1_ref[...] + gate2 * _rms(y, gpost_ref[...])


def _peer(h2t, u3, vt3, r1, e1, p2, e2, x1, mod3, g_post, B, S, tg, n_sub):
    D, N = h2t.shape
    n_sub_total = u3.shape[0]
    nblk = S // tg
    tok = lambda b, s, e: (b * nblk + s, 0)
    pitch = r1.shape[-1]
    tab_spec = pl.BlockSpec((1, PEER_HEADS, PEER_NKEYS // 2, pitch),
                            lambda b, s, e: (b * nblk + s, 0, 0, 0))
    rows_per_step = n_sub * PEER_SUB // PEER_NKEYS
    row_spec = pl.BlockSpec((1, PEER_HEADS, rows_per_step, pitch),
                            lambda b, s, e: (b * nblk + s, 0, e, 0))
    return pl.pallas_call(
        _peer_kernel,
        grid=(B, nblk, n_sub_total // n_sub),
        in_specs=[pl.BlockSpec((D, tg), lambda b, s, e: (0, b * nblk + s)),
                  pl.BlockSpec((n_sub, PEER_SUB, D), lambda b, s, e: (e, 0, 0)),
                  pl.BlockSpec((n_sub, D, PEER_SUB), lambda b, s, e: (e, 0, 0)),
                  row_spec, row_spec, tab_spec, tab_spec,
                  pl.BlockSpec((tg, D), tok),
                  pl.BlockSpec((1, 6, D), lambda b, s, e: (b, 0, 0)),
                  pl.BlockSpec((1, D), lambda b, s, e: (0, 0))],
        out_specs=pl.BlockSpec((tg, D), tok),
        out_shape=jax.ShapeDtypeStruct((N, D), F32),
        scratch_shapes=[pltpu.VMEM((D, tg), F32),
                        pltpu.VMEM((PEER_SUB, PEER_UNIT_TOKENS), BF16),
                        pltpu.VMEM((PEER_SUB, PEER_UNIT_TOKENS), BF16),
                        pltpu.VMEM((PEER_SUB, PEER_UNIT_TOKENS), BF16),
                        pltpu.VMEM((PEER_SUB, PEER_UNIT_TOKENS), BF16)],
        compiler_params=_params("arbitrary", "arbitrary", "arbitrary"),
        name="peer",
    )(h2t, u3, vt3, r1, e1, p2, e2, x1, mod3, g_post)


def _block(n, cap):
    b = min(n, cap)
    assert n % b == 0, (n, b)
    return b


def _layer(x, c, w_ada, b_ada, norm_pre_mix, norm_post_mix, w_in, w_alpha, b_alpha, gla_norm,
           conv_w, conv_b, lru_w_a, lru_b_a, lru_w_x, lru_b_x, lru_lambda, w_out,
           norm_pre_ffn, norm_post_ffn, peer_w_query, peer_sub_keys, peer_u, peer_v):
    B, S, D = x.shape
    N = B * S
    dk = w_alpha.shape[1]
    rank = w_alpha.shape[0]
    assert S % GLA_CHUNK == 0 and D % 128 == 0

    o_q, o_k, o_v, o_r, o_al, o_xl, o_xg, o_ga, o_gb = (
        0, dk, 2 * dk, 2 * dk + D, 2 * dk + 2 * D, 2 * dk + 2 * D + rank,
        2 * dk + 3 * D + rank, 2 * dk + 4 * D + rank, 2 * dk + 5 * D + rank)
    w_main = jnp.concatenate(
        [w_in[:, o_q:o_r], w_in[:, o_r:o_al], w_in[:, o_xl:]], axis=1).astype(BF16)
    w_al = jnp.pad(w_in[:, o_al:o_xl], ((0, 0), (0, GLA_RANK_PAD - rank))).astype(BF16)
    w_alpha_p = jnp.pad(w_alpha, ((0, GLA_RANK_PAD - rank), (0, 0))).astype(BF16)
    row = lambda a: a.reshape(1, -1).astype(F32)

    mod3 = _ada(c.astype(F32), w_ada.astype(F32), b_ada.astype(F32)).reshape(B, 6, D)
    x2 = x.reshape(N, D)

    tb = _block(S, 512)
    qk, v, r, xl, xg, ga, gb, la = _inproj(
        x2, mod3, row(norm_pre_mix), w_main, w_al, w_alpha_p, row(b_alpha), B, S, tb)

    y_gla = _gla(qk, v, r, la, row(gla_norm), B, S, _block(S, 256))
    y_lru = _lru(xl, xg, conv_w.astype(F32), row(conv_b), lru_w_a.astype(BF16), row(lru_b_a),
                 lru_w_x.astype(BF16), row(lru_b_x), row(lru_lambda), B, S, _block(S, 32))

    nh, _, nkeys, half = peer_sub_keys.shape
    keys = peer_sub_keys.reshape(nh * 2, nkeys, half).astype(BF16)
    x1, h2t, ss = _mix(y_gla, y_lru, ga, gb, x2, mod3, w_out.astype(BF16), row(norm_post_mix),
                      row(norm_pre_ffn), peer_w_query.astype(BF16), keys, B, S, _block(S, 512))

    tg = _block(S, 512)
    r1, e1, p2, e2 = _route(ss, tg)

    n_exp = peer_u.shape[0]
    u3 = peer_u.astype(BF16).reshape(n_exp // PEER_SUB, PEER_SUB, D)
    vt3 = peer_v.astype(BF16).reshape(n_exp // PEER_SUB, PEER_SUB, D).transpose(0, 2, 1)
    out = _peer(h2t, u3, vt3, r1, e1, p2, e2, x1, mod3, row(norm_post_ffn), B, S, tg, 8)
    return out.reshape(B, S, D)


def kernel(x, c, w_ada, b_ada, norm_pre_mix, norm_post_mix, w_in, w_alpha, b_alpha, gla_norm, conv_w, conv_b, lru_w_a, lru_b_a, lru_w_x, lru_b_x, lru_lambda, w_out, norm_pre_ffn, norm_post_ffn, peer_w_query, peer_sub_keys, peer_u, peer_v):
    depth = w_ada.shape[0]
    for l in range(depth):
        x = _layer(x, c, w_ada[l], b_ada[l], norm_pre_mix[l], norm_post_mix[l], w_in[l],
                   w_alpha[l], b_alpha[l], gla_norm[l], conv_w[l], conv_b[l], lru_w_a[l],
                   lru_b_a[l], lru_w_x[l], lru_b_x[l], lru_lambda[l], w_out[l],
                   norm_pre_ffn[l], norm_post_ffn[l], peer_w_query[l], peer_sub_keys[l],
                   peer_u[l], peer_v[l])
    return x
```

```python
import functools
import math

import jax
import jax.numpy as jnp
from jax import lax
from jax.experimental import pallas as pl
from jax.experimental.pallas import tpu as pltpu

F32 = jnp.float32
BF16 = jnp.bfloat16

EPS = 1e-6
GLA_HEADS = 4
GLA_RANK_PAD = 128
GLA_TAU = 16.0
GLA_CHUNK = 128
LRU_BLOCKS = 8
CONV_WIDTH = 4
LRU_C = 8.0
PEER_HEADS = 8
PEER_NKEYS = 128
PEER_TOPK = 16
PEER_SUB = 256
PEER_UNIT_TOKENS = 512
PEER_ROUTE_TOKENS = 256
PEER_TABLE_PAD = 128
SUBLANES = 8

VMEM_LIMIT = 56 * 1024 * 1024

NT_DIMS = (((1,), (1,)), ((), ()))
TN_DIMS = (((0,), (0,)), ((), ()))


def _dot(a, b):
    return jnp.dot(a, b, preferred_element_type=F32)


def _dot_nt(a, b):
    return lax.dot_general(a, b, NT_DIMS, preferred_element_type=F32)


def _dot_tn(a, b):
    return lax.dot_general(a, b, TN_DIMS, preferred_element_type=F32)


def _sigmoid(x):
    return 1.0 / (1.0 + jnp.exp(-x))


def _gelu_tanh(x):
    c0 = math.sqrt(2.0 / math.pi)
    return 0.5 * x * (1.0 + jnp.tanh(c0 * (x + 0.044715 * (x * x * x))))


def _rms(x, gain):
    ms = jnp.mean(x * x, axis=-1, keepdims=True)
    return x * lax.rsqrt(ms + EPS) * gain


def _params(*sem, flags=None):
    return pltpu.CompilerParams(dimension_semantics=sem, vmem_limit_bytes=VMEM_LIMIT, flags=flags)


def _ada_kernel(c_ref, w_ref, b_ref, o_ref):
    c = c_ref[...]
    s = c * _sigmoid(c)
    o_ref[...] = jnp.dot(s, w_ref[...], precision=lax.Precision.HIGHEST,
                         preferred_element_type=F32) + b_ref[...]


def _ada(c, w_ada, b_ada):
    B, D = c.shape
    n_out = w_ada.shape[1]
    blk = 1024
    return pl.pallas_call(
        _ada_kernel,
        grid=(n_out // blk,),
        in_specs=[pl.BlockSpec((B, D), lambda j: (0, 0)),
                  pl.BlockSpec((D, blk), lambda j: (0, j)),
                  pl.BlockSpec((1, blk), lambda j: (0, j))],
        out_specs=pl.BlockSpec((B, blk), lambda j: (0, j)),
        out_shape=jax.ShapeDtypeStruct((B, n_out), F32),
        compiler_params=_params("arbitrary"),
        name="ada",
    )(c, w_ada, b_ada.reshape(1, n_out))


def _inproj_kernel(x_ref, mod_ref, gain_ref, w_ref, wal_ref, walpha_ref, balpha_ref,
                   qk_ref, v_ref, r_ref, xl_ref, xg_ref, ga_ref, gb_ref, la_ref):
    D = x_ref.shape[-1]
    x = x_ref[...]
    shift = mod_ref[0, 0:1, :]
    scale = mod_ref[0, 1:2, :]
    h = (_rms(x, gain_ref[...]) * (1.0 + scale) + shift).astype(BF16)
    for j, o_ref in enumerate((qk_ref, v_ref, r_ref, xl_ref, xg_ref, ga_ref, gb_ref)):
        o_ref[...] = _dot(h, w_ref[:, j * D:(j + 1) * D]).astype(BF16)
    a_low = _dot(h, wal_ref[...]).astype(BF16)
    z = _dot(a_low, walpha_ref[...]) + balpha_ref[...]
    log_sig = jnp.minimum(z, 0.0) - jnp.log(1.0 + jnp.exp(-jnp.abs(z)))
    la_ref[...] = log_sig * (1.0 / GLA_TAU)


def _inproj(x2, mod3, gain, w_main, w_al, w_alpha, b_alpha, B, S, tb):
    N, D = x2.shape
    dk = w_alpha.shape[1]
    nblk = S // tb
    tok = lambda b, s: (b * nblk + s, 0)
    const = lambda b, s: (0, 0)
    big = jax.ShapeDtypeStruct((N, D), BF16)
    return pl.pallas_call(
        _inproj_kernel,
        grid=(B, nblk),
        in_specs=[pl.BlockSpec((tb, D), tok),
                  pl.BlockSpec((1, 6, D), lambda b, s: (b, 0, 0)),
                  pl.BlockSpec((1, D), const),
                  pl.BlockSpec(w_main.shape, const, pipeline_mode=pl.Buffered(1)),
                  pl.BlockSpec(w_al.shape, const),
                  pl.BlockSpec(w_alpha.shape, const),
                  pl.BlockSpec((1, dk), const)],
        out_specs=[pl.BlockSpec((tb, D), tok)] * 7 + [pl.BlockSpec((tb, dk), tok)],
        out_shape=[big] * 7 + [jax.ShapeDtypeStruct((N, dk), F32)],
        compiler_params=_params("arbitrary", "arbitrary"),
        name="inproj",
    )(x2, mod3, gain, w_main, w_al, w_alpha, b_alpha)


def _gla_kernel(qk_ref, v_ref, r_ref, la_ref, gn_ref, tri_ref, o_ref, st_ref):
    tc = qk_ref.shape[0]
    dk_all = la_ref.shape[-1]
    hdk = dk_all // GLA_HEADS
    hdv = v_ref.shape[-1] // GLA_HEADS
    C = GLA_CHUNK
    n_c = tc // C
    mid = C // 2 - 1

    @pl.when(pl.program_id(1) == 0)
    def _():
        st_ref[...] = jnp.zeros_like(st_ref)

    la = la_ref[...]
    p0 = la.astype(BF16)
    rem = la - p0.astype(F32)
    p1 = rem.astype(BF16)
    p2 = (rem - p1.astype(F32)).astype(BF16)
    tri = tri_ref[...]
    cum = _dot(tri, p0) + _dot(tri, p1) + _dot(tri, p2)

    def per_chunk_row(r):
        return jnp.concatenate(
            [jnp.broadcast_to(cum[c * C + r:c * C + r + 1], (C, dk_all)) for c in range(n_c)], axis=0)

    cm = per_chunk_row(mid)
    cl = per_chunk_row(C - 1)
    qt = qk_ref[:, 0:dk_all].astype(F32) * (hdk ** -0.5) * jnp.exp(cum - cm)
    kt = qk_ref[:, dk_all:2 * dk_all].astype(F32) * jnp.exp(cm - cum)
    qi = (qt * jnp.exp(cm)).astype(BF16)
    kd = (kt * jnp.exp(cl - cm)).astype(BF16)
    qt = qt.astype(BF16)
    kt = kt.astype(BF16)

    row = lax.broadcasted_iota(jnp.int32, (C, C), 0)
    col = lax.broadcasted_iota(jnp.int32, (C, C), 1)
    causal = row >= col
    pairs = [(c, h) for c in range(n_c) for h in range(GLA_HEADS)]
    rows = lambda c: slice(c * C, (c + 1) * C)
    ks = lambda h: slice(h * hdk, (h + 1) * hdk)
    vs = lambda h: slice(h * hdv, (h + 1) * hdv)

    scores = {p: jnp.where(causal, _dot_nt(qt[rows(p[0]), ks(p[1])], kt[rows(p[0]), ks(p[1])]),
                           0.0).astype(BF16) for p in pairs}
    intra = {p: _dot(scores[p], v_ref[rows(p[0]), vs(p[1])]) for p in pairs}
    update = {p: _dot_tn(v_ref[rows(p[0]), vs(p[1])], kd[rows(p[0]), ks(p[1])]) for p in pairs}
    state_in = {}
    for h in range(GLA_HEADS):
        st = st_ref[h]
        for c in range(n_c):
            state_in[(c, h)] = st.astype(BF16)
            e_last = jnp.exp(cum[(c + 1) * C - 1:(c + 1) * C, ks(h)])
            st = st * e_last + update[(c, h)]
        st_ref[h] = st
    for c, h in pairs:
        o = intra[(c, h)] + _dot_nt(qi[rows(c), ks(h)], state_in[(c, h)])
        y = _rms(o, gn_ref[:, vs(h)])
        rr = r_ref[rows(c), vs(h)].astype(F32)
        o_ref[rows(c), vs(h)] = (y * (rr * _sigmoid(rr))).astype(BF16)


def _gla(qk, v, r, la, gla_norm, B, S, tc):
    N, D = v.shape
    dk = la.shape[1]
    nblk = S // tc
    tok = lambda b, s: (b * nblk + s, 0)
    const = lambda b, s: (0, 0)
    idx = jnp.arange(tc)
    tri = ((idx[:, None] >= idx[None, :]) &
           (idx[:, None] // GLA_CHUNK == idx[None, :] // GLA_CHUNK)).astype(BF16)
    return pl.pallas_call(
        _gla_kernel,
        grid=(B, nblk),
        in_specs=[pl.BlockSpec((tc, 2 * dk), tok),
                  pl.BlockSpec((tc, D), tok),
                  pl.BlockSpec((tc, D), tok),
                  pl.BlockSpec((tc, dk), tok),
                  pl.BlockSpec((1, D), const),
                  pl.BlockSpec((tc, tc), const)],
        out_specs=pl.BlockSpec((tc, D), tok),
        out_shape=jax.ShapeDtypeStruct((N, D), BF16),
        scratch_shapes=[pltpu.VMEM((GLA_HEADS, D // GLA_HEADS, dk // GLA_HEADS), F32)],
        compiler_params=_params("arbitrary", "arbitrary"),
        name="gla",
    )(qk, v, r, la, gla_norm, tri)


def _lru_kernel(xl_ref, xg_ref, perm_ref, cw_ref, cb_ref, wa_ref, ba_ref, wx_ref, bx_ref,
                lam_ref, o_ref, xbuf_ref, h_ref):
    nb, tt, W = xl_ref.shape
    rows = nb * tt
    bw = W // LRU_BLOCKS
    tail = (CONV_WIDTH - 1) * nb

    @pl.when(pl.program_id(0) == 0)
    def _():
        xbuf_ref[0:tail, :] = jnp.zeros((tail, W), F32)
        h_ref[...] = jnp.zeros_like(h_ref)

    perm = perm_ref[...]
    xl = _dot(perm, xl_ref[...].reshape(rows, W))
    xg = _dot(perm, xg_ref[...].reshape(rows, W))

    xbuf_ref[tail:tail + rows, :] = xl
    xc = cb_ref[...]
    for j in range(CONV_WIDTH):
        xc = xc + cw_ref[j:j + 1, :] * xbuf_ref[j * nb:j * nb + rows, :]
    xbuf_ref[0:tail, :] = xbuf_ref[rows:rows + tail, :]

    xcb = xc.astype(BF16)
    gr_parts, gi_parts = [], []
    for n in range(LRU_BLOCKS):
        blk = xcb[:, n * bw:(n + 1) * bw]
        gr_parts.append(_dot(blk, wa_ref[n]))
        gi_parts.append(_dot(blk, wx_ref[n]))
    gate_r = _sigmoid(jnp.concatenate(gr_parts, axis=1) + ba_ref[...])
    gate_i = _sigmoid(jnp.concatenate(gi_parts, axis=1) + bx_ref[...])
    neg_lam = -lam_ref[...]
    softplus = jnp.maximum(neg_lam, 0.0) + jnp.log(1.0 + jnp.exp(-jnp.abs(neg_lam)))
    log_a = (-LRU_C) * gate_r * softplus
    a = jnp.exp(log_a)
    u = jnp.sqrt(jnp.maximum(1.0 - jnp.exp(2.0 * log_a), 0.0)) * (gate_i * xc)

    h = h_ref[...]
    steps = []
    for t in range(tt):
        grp = slice(t * nb, (t + 1) * nb)
        h = a[grp] * h + u[grp]
        steps.append(h)
    h_ref[...] = h
    y = (jnp.concatenate(steps, axis=0) * _gelu_tanh(xg)).astype(BF16)
    y = lax.dot_general(perm, y, TN_DIMS, preferred_element_type=F32).astype(BF16)
    o_ref[...] = y.reshape(nb, tt, W)


def _lru(xl, xg, conv_w, conv_b, wa, ba, wx, bx, lam, B, S, tt):
    N, W = xl.shape
    assert B == SUBLANES, "the time-major layout places the batch on the sublane axis"
    rows = B * tt
    r = jnp.arange(rows)
    perm = (r[None, :] == (r[:, None] % B) * tt + r[:, None] // B).astype(BF16)
    blk = lambda s: (0, s, 0)
    const2 = lambda s: (0, 0)
    const3 = lambda s: (0, 0, 0)
    out = pl.pallas_call(
        _lru_kernel,
        grid=(S // tt,),
        in_specs=[pl.BlockSpec((B, tt, W), blk),
                  pl.BlockSpec((B, tt, W), blk),
                  pl.BlockSpec((rows, rows), const2),
                  pl.BlockSpec(conv_w.shape, const2),
                  pl.BlockSpec((1, W), const2),
                  pl.BlockSpec(wa.shape, const3),
                  pl.BlockSpec((1, W), const2),
                  pl.BlockSpec(wx.shape, const3),
                  pl.BlockSpec((1, W), const2),
                  pl.BlockSpec((1, W), const2)],
        out_specs=pl.BlockSpec((B, tt, W), blk),
        out_shape=jax.ShapeDtypeStruct((B, S, W), BF16),
        scratch_shapes=[pltpu.VMEM((rows + (CONV_WIDTH - 1) * B, W), F32),
                        pltpu.VMEM((B, W), F32)],
        compiler_params=_params("arbitrary"),
        name="lru",
    )(xl.reshape(B, S, W), xg.reshape(B, S, W), perm, conv_w, conv_b, wa, ba, wx, bx, lam)
    return out.reshape(N, W)


def _mix_kernel(yg_ref, yl_ref, ga_ref, gb_ref, x_ref, mod_ref, wout_ref, gpost_ref,
                gpre_ref, wq_ref, keys_ref, x1_ref, h2t_ref, ss_ref):
    merged = (_sigmoid(ga_ref[...].astype(F32)) * yg_ref[...].astype(F32)
              + _sigmoid(gb_ref[...].astype(F32)) * yl_ref[...].astype(F32))
    y = _dot(merged.astype(BF16), wout_ref[...])
    gate1 = mod_ref[0, 2:3, :]
    x1 = x_ref[...] + gate1 * _rms(y, gpost_ref[...])
    x1_ref[...] = x1
    shift2 = mod_ref[0, 3:4, :]
    scale2 = mod_ref[0, 4:5, :]
    h2f = _rms(x1, gpre_ref[...]) * (1.0 + scale2) + shift2
    h2 = h2f.astype(BF16)
    h2t_ref[...] = h2f.T.astype(BF16)
    qry = _dot(h2, wq_ref[...]).astype(BF16)
    half = keys_ref.shape[-1]
    for hp in range(keys_ref.shape[0]):
        ss_ref[hp] = _dot_nt(keys_ref[hp], qry[:, hp * half:(hp + 1) * half])


def _mix(yg, yl, ga, gb, x2, mod3, w_out, g_post, g_pre, w_q, keys, B, S, te):
    N, D = x2.shape
    nblk = S // te
    tok = lambda b, s: (b * nblk + s, 0)
    const2 = lambda b, s: (0, 0)
    nhp, nkeys, _ = keys.shape
    return pl.pallas_call(
        _mix_kernel,
        grid=(B, nblk),
        in_specs=[pl.BlockSpec((te, D), tok)] * 5 + [
            pl.BlockSpec((1, 6, D), lambda b, s: (b, 0, 0)),
            pl.BlockSpec(w_out.shape, const2),
            pl.BlockSpec((1, D), const2),
            pl.BlockSpec((1, D), const2),
            pl.BlockSpec(w_q.shape, const2),
            pl.BlockSpec(keys.shape, lambda b, s: (0, 0, 0))],
        out_specs=[pl.BlockSpec((te, D), tok),
                   pl.BlockSpec((D, te), lambda b, s: (0, b * nblk + s)),
                   pl.BlockSpec((nhp, nkeys, te), lambda b, s: (0, 0, b * nblk + s))],
        out_shape=[jax.ShapeDtypeStruct((N, D), F32),
                   jax.ShapeDtypeStruct((D, N), BF16),
                   jax.ShapeDtypeStruct((nhp, nkeys, N), F32)],
        compiler_params=_params("arbitrary", "arbitrary"),
        name="mix",
    )(yg, yl, ga, gb, x2, mod3, w_out, g_post, g_pre, w_q, keys)


def _oddeven_merge_sort_pairs(n):
    pairs = []
    p = 1
    while p < n:
        k = p
        while k >= 1:
            for j in range(k % p, n - k, 2 * k):
                for i in range(min(k, n - j - k)):
                    if (i + j) // (2 * p) == (i + j + k) // (2 * p):
                        pairs.append((i + j, i + j + k))
            k //= 2
        p *= 2
    return pairs


_SORT16 = _oddeven_merge_sort_pairs(PEER_TOPK)


def _sort_desc(vals):
    vals = list(vals)
    for i, j in _SORT16:
        hi = jnp.maximum(vals[i], vals[j])
        lo = jnp.minimum(vals[i], vals[j])
        vals[i], vals[j] = hi, lo
    return vals


def _bitonic_merge_desc(vals):
    vals = list(vals)
    n = len(vals)
    d = n // 2
    while d >= 1:
        for i in range(n):
            if (i & d) == 0:
                hi = jnp.maximum(vals[i], vals[i + d])
                lo = jnp.minimum(vals[i], vals[i + d])
                vals[i], vals[i + d] = hi, lo
        d //= 2
    return vals


def _merge_across_sublanes(vals):
    n = len(vals)
    for shift in (4, 2, 1):
        partner = [pltpu.roll(v, shift, 0) for v in vals]
        vals = _bitonic_merge_desc([jnp.maximum(vals[i], partner[n - 1 - i]) for i in range(n)])
    return vals


def _top16_sorted(s_ref, hp, lanes):
    groups = [s_ref[hp, SUBLANES * i:SUBLANES * (i + 1), lanes]
              for i in range(PEER_NKEYS // SUBLANES)]
    return _merge_across_sublanes(_sort_desc(groups))


def _dup_bf16_bits(x):
    hi = pltpu.bitcast(x.astype(BF16).astype(F32), jnp.uint32)
    return hi | lax.shift_right_logical(hi, jnp.full(hi.shape, 16, jnp.uint32))


def _route_kernel(ss_ref, r1_ref, e1_ref, p2_ref, e2_ref):
    tokens = ss_ref.shape[-1]
    tf = PEER_ROUTE_TOKENS
    K = PEER_TOPK
    ngroups = PEER_NKEYS // SUBLANES
    sub = lax.broadcasted_iota(jnp.int32, (SUBLANES, tf), 0)

    def route_lanes(h, lanes):
        a = _top16_sorted(ss_ref, 2 * h, lanes)
        b = _top16_sorted(ss_ref, 2 * h + 1, lanes)
        a_lo = a[0]
        a_hi = a[SUBLANES]
        for i in range(1, SUBLANES):
            a_lo = jnp.where(sub == i, a[i], a_lo)
            a_hi = jnp.where(sub == i, a[SUBLANES + i], a_hi)
        cand = [a_lo + b[j] for j in range(K)]
        extra = a_hi + b[0]
        ins = [jnp.maximum(cand[0], extra)]
        for j in range(1, K):
            ins.append(jnp.maximum(cand[j], jnp.minimum(cand[j - 1], extra)))
        top = _merge_across_sublanes(ins)
        thr = top[K - 1]
        smax = top[0]
        z = jnp.exp(top[0] - smax)
        for j in range(1, K):
            z = z + jnp.exp(top[j] - smax)
        inv_z = 1.0 / z

        cap = []
        for j in range(K):
            cnt = (jnp.where(cand[j] >= thr, 1.0, 0.0)
                   + jnp.where(a_hi + b[j] >= thr, 1.0, 0.0))
            for shift in (4, 2, 1):
                cnt = cnt + pltpu.roll(cnt, shift, 0)
            cap.append(cnt)

        r1_rows, e1_rows, p2_rows, e2_rows = [], [], [], []
        for i in range(ngroups):
            s1 = ss_ref[2 * h, SUBLANES * i:SUBLANES * (i + 1), lanes]
            s2 = ss_ref[2 * h + 1, SUBLANES * i:SUBLANES * (i + 1), lanes]
            rank1 = jnp.full_like(s1, K + 1.0)
            cap2 = jnp.zeros_like(s2)
            for r in reversed(range(K)):
                rank1 = jnp.where(s1 == a[r], r + 1.0, rank1)
                cap2 = jnp.where(s2 == b[r], cap[r], cap2)
            r1_rows.append(rank1)
            p2_rows.append(cap2)
            e1_rows.append(jnp.exp(s1 - a[0]))
            e2_rows.append(jnp.exp(s2 - b[0]) * (0.5 * inv_z))
        r1_ref[0, h, :, lanes] = _dup_bf16_bits(jnp.concatenate(r1_rows, axis=0))
        e1_ref[0, h, :, lanes] = _dup_bf16_bits(jnp.concatenate(e1_rows, axis=0))
        p2_ref[0, h, :, lanes] = pltpu.bitcast(
            jnp.concatenate(p2_rows, axis=0).astype(BF16), jnp.uint32)
        e2_ref[0, h, :, lanes] = pltpu.bitcast(
            jnp.concatenate(e2_rows, axis=0).astype(BF16), jnp.uint32)

    def head_body(h, carry):
        for part in range(tokens // tf):
            route_lanes(h, slice(part * tf, (part + 1) * tf))
        for ref in (r1_ref, e1_ref, p2_ref, e2_ref):
            ref[0, h, :, tokens:] = jnp.zeros((ref.shape[2], ref.shape[3] - tokens), jnp.uint32)
        return carry

    lax.fori_loop(0, PEER_HEADS, head_body, 0)


def _route(ss, tg):
    nhp, nkeys, N = ss.shape
    pitch = tg + PEER_TABLE_PAD
    spec_in = pl.BlockSpec((nhp, nkeys, tg), lambda t: (0, 0, t))
    spec_out = pl.BlockSpec((1, PEER_HEADS, nkeys, pitch), lambda t: (t, 0, 0, 0))
    spec_packed = pl.BlockSpec((1, PEER_HEADS, nkeys // 2, pitch), lambda t: (t, 0, 0, 0))
    f32s = jax.ShapeDtypeStruct((N // tg, PEER_HEADS, nkeys, pitch), jnp.uint32)
    bf16s = jax.ShapeDtypeStruct((N // tg, PEER_HEADS, nkeys // 2, pitch), jnp.uint32)
    return pl.pallas_call(
        _route_kernel,
        grid=(N // tg,),
        in_specs=[spec_in],
        out_specs=[spec_out, spec_out, spec_packed, spec_packed],
        out_shape=[f32s, f32s, bf16s, bf16s],
        compiler_params=_params("arbitrary"),
        name="route",
    )(ss)


def _peer_kernel(h2t_ref, u_ref, vt_ref, r1_ref, e1_ref, p2_ref, e2_ref, x1_ref, mod_ref,
                 gpost_ref, o_ref, acc_ref, xa_ref, xb_ref, wa_ref, wb_ref):
    e = pl.program_id(2)
    n_e = pl.num_programs(2)
    n_sub = u_ref.shape[0]
    tg = h2t_ref.shape[1]
    rows_per_sub = PEER_SUB // PEER_NKEYS
    chunk = 2 * SUBLANES
    n_chunks = PEER_NKEYS // chunk
    c0 = math.sqrt(2.0 / math.pi)

    @pl.when(e == 0)
    def _():
        acc_ref[...] = jnp.zeros_like(acc_ref)

    xbufs = (xa_ref, xb_ref)
    wbufs = (wa_ref, wb_ref)

    tw = xa_ref.shape[1]
    n_split = tg // tw

    def unit(k):
        return k // n_split, slice((k % n_split) * tw, (k % n_split + 1) * tw)

    def scores_into(k):
        sb, cols = unit(k)
        xbufs[k % 2][...] = _dot(u_ref[sb], h2t_ref[:, cols]).astype(BF16)

    def stack(parts):
        return jnp.concatenate(parts, axis=0)

    def tile(ref, h, c, lanes):
        return pltpu.bitcast(ref[0, h, c * SUBLANES:(c + 1) * SUBLANES, lanes], BF16)

    def gates(k):
        sb, cols = unit(k)
        xbuf_ref = xbufs[k % 2]
        wbuf_ref = wbufs[k % 2]
        i1 = sb * rows_per_sub

        def row_bcast(ref, h, lanes):
            return stack([pltpu.bitcast(
                jnp.broadcast_to(ref[0, h, i1 + j:i1 + j + 1, lanes], (SUBLANES, 128)), BF16)
                for j in range(rows_per_sub)])

        zero = jnp.zeros((rows_per_sub * chunk, 128), BF16)
        for lg in range(tw // 128):
            lanes = slice(cols.start + lg * 128, cols.start + (lg + 1) * 128)
            local = slice(lg * 128, (lg + 1) * 128)
            g = [zero] * n_chunks
            for h in range(PEER_HEADS):
                rank1 = row_bcast(r1_ref, h, lanes)
                gate1 = row_bcast(e1_ref, h, lanes)
                for c in range(n_chunks):
                    cap2 = stack([tile(p2_ref, h, c, lanes)] * rows_per_sub)
                    gate2 = stack([tile(e2_ref, h, c, lanes)] * rows_per_sub)
                    g[c] = g[c] + jnp.where(rank1 <= cap2, gate2, 0.0) * gate1
            for c in range(n_chunks):
                x = stack([xbuf_ref[j * PEER_NKEYS + c * chunk:j * PEER_NKEYS + (c + 1) * chunk, local]
                           for j in range(rows_per_sub)])
                w = x * (1.0 + jnp.tanh(x * (c0 + (c0 * 0.044715) * (x * x)))) * g[c]
                for j in range(rows_per_sub):
                    wbuf_ref[j * PEER_NKEYS + c * chunk:j * PEER_NKEYS + (c + 1) * chunk, local] = (
                        w[j * chunk:(j + 1) * chunk])

    def accumulate(k):
        sb, cols = unit(k)
        acc_ref[:, cols] += _dot(vt_ref[sb], wbufs[k % 2][...])

    n_units = n_sub * n_split
    scores_into(0)
    for k in range(n_units):
        if k + 1 < n_units:
            scores_into(k + 1)
        if k >= 1:
            accumulate(k - 1)
        gates(k)
    accumulate(n_units - 1)

    @pl.when(e == n_e - 1)
    def _():
        y = acc_ref[...].T
        gate2 = mod_ref[0, 5:6, :]
        o_ref[...] = x1_ref[...] + gate2 * _rms(y, gpost_ref[...])


def _peer(h2t, u3, vt3, r1, e1, p2, e2, x1, mod3, g_post, B, S, tg, n_sub):
    D, N = h2t.shape
    n_sub_total = u3.shape[0]
    nblk = S // tg
    tok = lambda b, s, e: (b * nblk + s, 0)
    pitch = r1.shape[-1]
    tab_spec = pl.BlockSpec((1, PEER_HEADS, PEER_NKEYS // 2, pitch),
                            lambda b, s, e: (b * nblk + s, 0, 0, 0))
    rows_per_step = n_sub * PEER_SUB // PEER_NKEYS
    row_spec = pl.BlockSpec((1, PEER_HEADS, rows_per_step, pitch),
                            lambda b, s, e: (b * nblk + s, 0, e, 0))
    return pl.pallas_call(
        _peer_kernel,
        grid=(B, nblk, n_sub_total // n_sub),
        in_specs=[pl.BlockSpec((D, tg), lambda b, s, e: (0, b * nblk + s)),
                  pl.BlockSpec((n_sub, PEER_SUB, D), lambda b, s, e: (e, 0, 0)),
                  pl.BlockSpec((n_sub, D, PEER_SUB), lambda b, s, e: (e, 0, 0)),
                  row_spec, row_spec, tab_spec, tab_spec,
                  pl.BlockSpec((tg, D), tok),
                  pl.BlockSpec((1, 6, D), lambda b, s, e: (b, 0, 0)),
                  pl.BlockSpec((1, D), lambda b, s, e: (0, 0))],
        out_specs=pl.BlockSpec((tg, D), tok),
        out_shape=jax.ShapeDtypeStruct((N, D), F32),
        scratch_shapes=[pltpu.VMEM((D, tg), F32),
                        pltpu.VMEM((PEER_SUB, PEER_UNIT_TOKENS), BF16),
                        pltpu.VMEM((PEER_SUB, PEER_UNIT_TOKENS), BF16),
                        pltpu.VMEM((PEER_SUB, PEER_UNIT_TOKENS), BF16),
                        pltpu.VMEM((PEER_SUB, PEER_UNIT_TOKENS), BF16)],
        compiler_params=_params("arbitrary", "arbitrary", "arbitrary"),
        name="peer",
    )(h2t, u3, vt3, r1, e1, p2, e2, x1, mod3, g_post)


def _block(n, cap):
    b = min(n, cap)
    assert n % b == 0, (n, b)
    return b


def _layer(x, c, w_ada, b_ada, norm_pre_mix, norm_post_mix, w_in, w_alpha, b_alpha, gla_norm,
           conv_w, conv_b, lru_w_a, lru_b_a, lru_w_x, lru_b_x, lru_lambda, w_out,
           norm_pre_ffn, norm_post_ffn, peer_w_query, peer_sub_keys, peer_u, peer_v):
    B, S, D = x.shape
    N = B * S
    dk = w_alpha.shape[1]
    rank = w_alpha.shape[0]
    assert S % GLA_CHUNK == 0 and D % 128 == 0

    o_q, o_k, o_v, o_r, o_al, o_xl, o_xg, o_ga, o_gb = (
        0, dk, 2 * dk, 2 * dk + D, 2 * dk + 2 * D, 2 * dk + 2 * D + rank,
        2 * dk + 3 * D + rank, 2 * dk + 4 * D + rank, 2 * dk + 5 * D + rank)
    w_main = jnp.concatenate(
        [w_in[:, o_q:o_r], w_in[:, o_r:o_al], w_in[:, o_xl:]], axis=1).astype(BF16)
    w_al = jnp.pad(w_in[:, o_al:o_xl], ((0, 0), (0, GLA_RANK_PAD - rank))).astype(BF16)
    w_alpha_p = jnp.pad(w_alpha, ((0, GLA_RANK_PAD - rank), (0, 0))).astype(BF16)
    row = lambda a: a.reshape(1, -1).astype(F32)

    mod3 = _ada(c.astype(F32), w_ada.astype(F32), b_ada.astype(F32)).reshape(B, 6, D)
    x2 = x.reshape(N, D)

    tb = _block(S, 512)
    qk, v, r, xl, xg, ga, gb, la = _inproj(
        x2, mod3, row(norm_pre_mix), w_main, w_al, w_alpha_p, row(b_alpha), B, S, tb)

    y_gla = _gla(qk, v, r, la, row(gla_norm), B, S, _block(S, 256))
    y_lru = _lru(xl, xg, conv_w.astype(F32), row(conv_b), lru_w_a.astype(BF16), row(lru_b_a),
                 lru_w_x.astype(BF16), row(lru_b_x), row(lru_lambda), B, S, _block(S, 32))

    nh, _, nkeys, half = peer_sub_keys.shape
    keys = peer_sub_keys.reshape(nh * 2, nkeys, half).astype(BF16)
    x1, h2t, ss = _mix(y_gla, y_lru, ga, gb, x2, mod3, w_out.astype(BF16), row(norm_post_mix),
                      row(norm_pre_ffn), peer_w_query.astype(BF16), keys, B, S, _block(S, 512))

    tg = _block(S, 512)
    r1, e1, p2, e2 = _route(ss, tg)

    n_exp = peer_u.shape[0]
    u3 = peer_u.astype(BF16).reshape(n_exp // PEER_SUB, PEER_SUB, D)
    vt3 = peer_v.astype(BF16).reshape(n_exp // PEER_SUB, PEER_SUB, D).transpose(0, 2, 1)
    out = _peer(h2t, u3, vt3, r1, e1, p2, e2, x1, mod3, row(norm_post_ffn), B, S, tg, 8)
    return out.reshape(B, S, D)


def kernel(x, c, w_ada, b_ada, norm_pre_mix, norm_post_mix, w_in, w_alpha, b_alpha, gla_norm, conv_w, conv_b, lru_w_a, lru_b_a, lru_w_x, lru_b_x, lru_lambda, w_out, norm_pre_ffn, norm_post_ffn, peer_w_query, peer_sub_keys, peer_u, peer_v):
    depth = w_ada.shape[0]
    for l in range(depth):
        x = _layer(x, c, w_ada[l], b_ada[l], norm_pre_mix[l], norm_post_mix[l], w_in[l],
                   w_alpha[l], b_alpha[l], gla_norm[l], conv_w[l], conv_b[l], lru_w_a[l],
                   lru_b_a[l], lru_w_x[l], lru_b_x[l], lru_lambda[l], w_out[l],
                   norm_pre_ffn[l], norm_post_ffn[l], peer_w_query[l], peer_sub_keys[l],
                   peer_u[l], peer_v[l])
    return x
```

```python
import functools
import math

import jax
import jax.numpy as jnp
from jax import lax
from jax.experimental import pallas as pl
from jax.experimental.pallas import tpu as pltpu

F32 = jnp.float32
BF16 = jnp.bfloat16

EPS = 1e-6
GLA_HEADS = 4
GLA_RANK_PAD = 128
GLA_TAU = 16.0
GLA_CHUNK = 128
LRU_BLOCKS = 8
CONV_WIDTH = 4
LRU_C = 8.0
PEER_HEADS = 8
PEER_NKEYS = 128
PEER_TOPK = 16
PEER_SUB = 512
PEER_STEP_EXPERTS = 2048
PEER_UNIT_TOKENS = 512
PEER_ROUTE_TOKENS = 256
PEER_TABLE_PAD = 128
SUBLANES = 8

VMEM_LIMIT = 56 * 1024 * 1024

NT_DIMS = (((1,), (1,)), ((), ()))
TN_DIMS = (((0,), (0,)), ((), ()))


def _dot(a, b):
    return jnp.dot(a, b, preferred_element_type=F32)


def _dot_nt(a, b):
    return lax.dot_general(a, b, NT_DIMS, preferred_element_type=F32)


def _dot_tn(a, b):
    return lax.dot_general(a, b, TN_DIMS, preferred_element_type=F32)


def _sigmoid(x):
    return 1.0 / (1.0 + jnp.exp(-x))


def _gelu_tanh(x):
    c0 = math.sqrt(2.0 / math.pi)
    return 0.5 * x * (1.0 + jnp.tanh(c0 * (x + 0.044715 * (x * x * x))))


def _rms(x, gain):
    ms = jnp.mean(x * x, axis=-1, keepdims=True)
    return x * lax.rsqrt(ms + EPS) * gain


def _params(*sem, flags=None):
    return pltpu.CompilerParams(dimension_semantics=sem, vmem_limit_bytes=VMEM_LIMIT, flags=flags)


def _ada_kernel(c_ref, w_ref, b_ref, o_ref):
    c = c_ref[...]
    s = c * _sigmoid(c)
    o_ref[...] = jnp.dot(s, w_ref[...], precision=lax.Precision.HIGHEST,
                         preferred_element_type=F32) + b_ref[...]


def _ada(c, w_ada, b_ada):
    B, D = c.shape
    n_out = w_ada.shape[1]
    blk = 1024
    return pl.pallas_call(
        _ada_kernel,
        grid=(n_out // blk,),
        in_specs=[pl.BlockSpec((B, D), lambda j: (0, 0)),
                  pl.BlockSpec((D, blk), lambda j: (0, j)),
                  pl.BlockSpec((1, blk), lambda j: (0, j))],
        out_specs=pl.BlockSpec((B, blk), lambda j: (0, j)),
        out_shape=jax.ShapeDtypeStruct((B, n_out), F32),
        compiler_params=_params("arbitrary"),
        name="ada",
    )(c, w_ada, b_ada.reshape(1, n_out))


def _inproj_kernel(x_ref, mod_ref, gain_ref, w_ref, wal_ref, walpha_ref, balpha_ref,
                   qk_ref, v_ref, r_ref, xl_ref, xg_ref, ga_ref, gb_ref, la_ref):
    D = x_ref.shape[-1]
    x = x_ref[...]
    shift = mod_ref[0, 0:1, :]
    scale = mod_ref[0, 1:2, :]
    h = (_rms(x, gain_ref[...]) * (1.0 + scale) + shift).astype(BF16)
    for j, o_ref in enumerate((qk_ref, v_ref, r_ref, xl_ref, xg_ref, ga_ref, gb_ref)):
        o_ref[...] = _dot(h, w_ref[:, j * D:(j + 1) * D]).astype(BF16)
    a_low = _dot(h, wal_ref[...]).astype(BF16)
    z = _dot(a_low, walpha_ref[...]) + balpha_ref[...]
    log_sig = jnp.minimum(z, 0.0) - jnp.log(1.0 + jnp.exp(-jnp.abs(z)))
    la_ref[...] = log_sig * (1.0 / GLA_TAU)


def _inproj(x2, mod3, gain, w_main, w_al, w_alpha, b_alpha, B, S, tb):
    N, D = x2.shape
    dk = w_alpha.shape[1]
    nblk = S // tb
    tok = lambda b, s: (b * nblk + s, 0)
    const = lambda b, s: (0, 0)
    big = jax.ShapeDtypeStruct((N, D), BF16)
    return pl.pallas_call(
        _inproj_kernel,
        grid=(B, nblk),
        in_specs=[pl.BlockSpec((tb, D), tok),
                  pl.BlockSpec((1, 6, D), lambda b, s: (b, 0, 0)),
                  pl.BlockSpec((1, D), const),
                  pl.BlockSpec(w_main.shape, const, pipeline_mode=pl.Buffered(1)),
                  pl.BlockSpec(w_al.shape, const),
                  pl.BlockSpec(w_alpha.shape, const),
                  pl.BlockSpec((1, dk), const)],
        out_specs=[pl.BlockSpec((tb, D), tok)] * 7 + [pl.BlockSpec((tb, dk), tok)],
        out_shape=[big] * 7 + [jax.ShapeDtypeStruct((N, dk), F32)],
        compiler_params=_params("arbitrary", "arbitrary"),
        name="inproj",
    )(x2, mod3, gain, w_main, w_al, w_alpha, b_alpha)


def _gla_kernel(qk_ref, v_ref, r_ref, la_ref, gn_ref, tri_ref, o_ref, st_ref):
    tc = qk_ref.shape[0]
    dk_all = la_ref.shape[-1]
    hdk = dk_all // GLA_HEADS
    hdv = v_ref.shape[-1] // GLA_HEADS
    C = GLA_CHUNK
    n_c = tc // C
    mid = C // 2 - 1

    @pl.when(pl.program_id(1) == 0)
    def _():
        st_ref[...] = jnp.zeros_like(st_ref)

    la = la_ref[...]
    p0 = la.astype(BF16)
    rem = la - p0.astype(F32)
    p1 = rem.astype(BF16)
    p2 = (rem - p1.astype(F32)).astype(BF16)
    tri = tri_ref[...]
    cum = _dot(tri, p0) + _dot(tri, p1) + _dot(tri, p2)

    def per_chunk_row(r):
        return jnp.concatenate(
            [jnp.broadcast_to(cum[c * C + r:c * C + r + 1], (C, dk_all)) for c in range(n_c)], axis=0)

    cm = per_chunk_row(mid)
    cl = per_chunk_row(C - 1)
    qt = qk_ref[:, 0:dk_all].astype(F32) * (hdk ** -0.5) * jnp.exp(cum - cm)
    kt = qk_ref[:, dk_all:2 * dk_all].astype(F32) * jnp.exp(cm - cum)
    qi = (qt * jnp.exp(cm)).astype(BF16)
    kd = (kt * jnp.exp(cl - cm)).astype(BF16)
    qt = qt.astype(BF16)
    kt = kt.astype(BF16)

    row = lax.broadcasted_iota(jnp.int32, (C, C), 0)
    col = lax.broadcasted_iota(jnp.int32, (C, C), 1)
    causal = row >= col
    pairs = [(c, h) for c in range(n_c) for h in range(GLA_HEADS)]
    rows = lambda c: slice(c * C, (c + 1) * C)
    ks = lambda h: slice(h * hdk, (h + 1) * hdk)
    vs = lambda h: slice(h * hdv, (h + 1) * hdv)

    scores = {p: jnp.where(causal, _dot_nt(qt[rows(p[0]), ks(p[1])], kt[rows(p[0]), ks(p[1])]),
                           0.0).astype(BF16) for p in pairs}
    intra = {p: _dot(scores[p], v_ref[rows(p[0]), vs(p[1])]) for p in pairs}
    update = {p: _dot_tn(v_ref[rows(p[0]), vs(p[1])], kd[rows(p[0]), ks(p[1])]) for p in pairs}
    state_in = {}
    for h in range(GLA_HEADS):
        st = st_ref[h]
        for c in range(n_c):
            state_in[(c, h)] = st.astype(BF16)
            e_last = jnp.exp(cum[(c + 1) * C - 1:(c + 1) * C, ks(h)])
            st = st * e_last + update[(c, h)]
        st_ref[h] = st
    for c, h in pairs:
        o = intra[(c, h)] + _dot_nt(qi[rows(c), ks(h)], state_in[(c, h)])
        y = _rms(o, gn_ref[:, vs(h)])
        rr = r_ref[rows(c), vs(h)].astype(F32)
        o_ref[rows(c), vs(h)] = (y * (rr * _sigmoid(rr))).astype(BF16)


def _gla(qk, v, r, la, gla_norm, B, S, tc):
    N, D = v.shape
    dk = la.shape[1]
    nblk = S // tc
    tok = lambda b, s: (b * nblk + s, 0)
    const = lambda b, s: (0, 0)
    idx = jnp.arange(tc)
    tri = ((idx[:, None] >= idx[None, :]) &
           (idx[:, None] // GLA_CHUNK == idx[None, :] // GLA_CHUNK)).astype(BF16)
    return pl.pallas_call(
        _gla_kernel,
        grid=(B, nblk),
        in_specs=[pl.BlockSpec((tc, 2 * dk), tok),
                  pl.BlockSpec((tc, D), tok),
                  pl.BlockSpec((tc, D), tok),
                  pl.BlockSpec((tc, dk), tok),
                  pl.BlockSpec((1, D), const),
                  pl.BlockSpec((tc, tc), const)],
        out_specs=pl.BlockSpec((tc, D), tok),
        out_shape=jax.ShapeDtypeStruct((N, D), BF16),
        scratch_shapes=[pltpu.VMEM((GLA_HEADS, D // GLA_HEADS, dk // GLA_HEADS), F32)],
        compiler_params=_params("arbitrary", "arbitrary"),
        name="gla",
    )(qk, v, r, la, gla_norm, tri)


def _lru_kernel(xl_ref, xg_ref, perm_ref, cw_ref, cb_ref, wa_ref, ba_ref, wx_ref, bx_ref,
                lam_ref, o_ref, xbuf_ref, h_ref):
    nb, tt, W = xl_ref.shape
    rows = nb * tt
    bw = W // LRU_BLOCKS
    tail = (CONV_WIDTH - 1) * nb

    @pl.when(pl.program_id(0) == 0)
    def _():
        xbuf_ref[0:tail, :] = jnp.zeros((tail, W), F32)
        h_ref[...] = jnp.zeros_like(h_ref)

    perm = perm_ref[...]
    xl = _dot(perm, xl_ref[...].reshape(rows, W))
    xg = _dot(perm, xg_ref[...].reshape(rows, W))

    xbuf_ref[tail:tail + rows, :] = xl
    xc = cb_ref[...]
    for j in range(CONV_WIDTH):
        xc = xc + cw_ref[j:j + 1, :] * xbuf_ref[j * nb:j * nb + rows, :]
    xbuf_ref[0:tail, :] = xbuf_ref[rows:rows + tail, :]

    xcb = xc.astype(BF16)
    gr_parts, gi_parts = [], []
    for n in range(LRU_BLOCKS):
        blk = xcb[:, n * bw:(n + 1) * bw]
        gr_parts.append(_dot(blk, wa_ref[n]))
        gi_parts.append(_dot(blk, wx_ref[n]))
    gate_r = _sigmoid(jnp.concatenate(gr_parts, axis=1) + ba_ref[...])
    gate_i = _sigmoid(jnp.concatenate(gi_parts, axis=1) + bx_ref[...])
    neg_lam = -lam_ref[...]
    softplus = jnp.maximum(neg_lam, 0.0) + jnp.log(1.0 + jnp.exp(-jnp.abs(neg_lam)))
    log_a = (-LRU_C) * gate_r * softplus
    a = jnp.exp(log_a)
    u = jnp.sqrt(jnp.maximum(1.0 - jnp.exp(2.0 * log_a), 0.0)) * (gate_i * xc)

    h = h_ref[...]
    steps = []
    for t in range(tt):
        grp = slice(t * nb, (t + 1) * nb)
        h = a[grp] * h + u[grp]
        steps.append(h)
    h_ref[...] = h
    y = (jnp.concatenate(steps, axis=0) * _gelu_tanh(xg)).astype(BF16)
    y = lax.dot_general(perm, y, TN_DIMS, preferred_element_type=F32).astype(BF16)
    o_ref[...] = y.reshape(nb, tt, W)


def _lru(xl, xg, conv_w, conv_b, wa, ba, wx, bx, lam, B, S, tt):
    N, W = xl.shape
    assert B == SUBLANES, "the time-major layout places the batch on the sublane axis"
    rows = B * tt
    r = jnp.arange(rows)
    perm = (r[None, :] == (r[:, None] % B) * tt + r[:, None] // B).astype(BF16)
    blk = lambda s: (0, s, 0)
    const2 = lambda s: (0, 0)
    const3 = lambda s: (0, 0, 0)
    out = pl.pallas_call(
        _lru_kernel,
        grid=(S // tt,),
        in_specs=[pl.BlockSpec((B, tt, W), blk),
                  pl.BlockSpec((B, tt, W), blk),
                  pl.BlockSpec((rows, rows), const2),
                  pl.BlockSpec(conv_w.shape, const2),
                  pl.BlockSpec((1, W), const2),
                  pl.BlockSpec(wa.shape, const3),
                  pl.BlockSpec((1, W), const2),
                  pl.BlockSpec(wx.shape, const3),
                  pl.BlockSpec((1, W), const2),
                  pl.BlockSpec((1, W), const2)],
        out_specs=pl.BlockSpec((B, tt, W), blk),
        out_shape=jax.ShapeDtypeStruct((B, S, W), BF16),
        scratch_shapes=[pltpu.VMEM((rows + (CONV_WIDTH - 1) * B, W), F32),
                        pltpu.VMEM((B, W), F32)],
        compiler_params=_params("arbitrary"),
        name="lru",
    )(xl.reshape(B, S, W), xg.reshape(B, S, W), perm, conv_w, conv_b, wa, ba, wx, bx, lam)
    return out.reshape(N, W)


def _mix_kernel(yg_ref, yl_ref, ga_ref, gb_ref, x_ref, mod_ref, wout_ref, gpost_ref,
                gpre_ref, wq_ref, keys_ref, x1_ref, h2t_ref, ss_ref):
    merged = (_sigmoid(ga_ref[...].astype(F32)) * yg_ref[...].astype(F32)
              + _sigmoid(gb_ref[...].astype(F32)) * yl_ref[...].astype(F32))
    y = _dot(merged.astype(BF16), wout_ref[...])
    gate1 = mod_ref[0, 2:3, :]
    x1 = x_ref[...] + gate1 * _rms(y, gpost_ref[...])
    x1_ref[...] = x1
    shift2 = mod_ref[0, 3:4, :]
    scale2 = mod_ref[0, 4:5, :]
    h2f = _rms(x1, gpre_ref[...]) * (1.0 + scale2) + shift2
    h2 = h2f.astype(BF16)
    h2t_ref[...] = h2f.T.astype(BF16)
    qry = _dot(h2, wq_ref[...]).astype(BF16)
    half = keys_ref.shape[-1]
    for hp in range(keys_ref.shape[0]):
        ss_ref[hp] = _dot_nt(keys_ref[hp], qry[:, hp * half:(hp + 1) * half])


def _mix(yg, yl, ga, gb, x2, mod3, w_out, g_post, g_pre, w_q, keys, B, S, te):
    N, D = x2.shape
    nblk = S // te
    tok = lambda b, s: (b * nblk + s, 0)
    const2 = lambda b, s: (0, 0)
    nhp, nkeys, _ = keys.shape
    return pl.pallas_call(
        _mix_kernel,
        grid=(B, nblk),
        in_specs=[pl.BlockSpec((te, D), tok)] * 5 + [
            pl.BlockSpec((1, 6, D), lambda b, s: (b, 0, 0)),
            pl.BlockSpec(w_out.shape, const2),
            pl.BlockSpec((1, D), const2),
            pl.BlockSpec((1, D), const2),
            pl.BlockSpec(w_q.shape, const2),
            pl.BlockSpec(keys.shape, lambda b, s: (0, 0, 0))],
        out_specs=[pl.BlockSpec((te, D), tok),
                   pl.BlockSpec((D, te), lambda b, s: (0, b * nblk + s)),
                   pl.BlockSpec((nhp, nkeys, te), lambda b, s: (0, 0, b * nblk + s))],
        out_shape=[jax.ShapeDtypeStruct((N, D), F32),
                   jax.ShapeDtypeStruct((D, N), BF16),
                   jax.ShapeDtypeStruct((nhp, nkeys, N), F32)],
        compiler_params=_params("arbitrary", "arbitrary"),
        name="mix",
    )(yg, yl, ga, gb, x2, mod3, w_out, g_post, g_pre, w_q, keys)


def _oddeven_merge_sort_pairs(n):
    pairs = []
    p = 1
    while p < n:
        k = p
        while k >= 1:
            for j in range(k % p, n - k, 2 * k):
                for i in range(min(k, n - j - k)):
                    if (i + j) // (2 * p) == (i + j + k) // (2 * p):
                        pairs.append((i + j, i + j + k))
            k //= 2
        p *= 2
    return pairs


_SORT16 = _oddeven_merge_sort_pairs(PEER_TOPK)


def _sort_desc(vals):
    vals = list(vals)
    for i, j in _SORT16:
        hi = jnp.maximum(vals[i], vals[j])
        lo = jnp.minimum(vals[i], vals[j])
        vals[i], vals[j] = hi, lo
    return vals


def _bitonic_merge_desc(vals):
    vals = list(vals)
    n = len(vals)
    d = n // 2
    while d >= 1:
        for i in range(n):
            if (i & d) == 0:
                hi = jnp.maximum(vals[i], vals[i + d])
                lo = jnp.minimum(vals[i], vals[i + d])
                vals[i], vals[i + d] = hi, lo
        d //= 2
    return vals


def _merge_across_sublanes(vals):
    n = len(vals)
    for shift in (4, 2, 1):
        partner = [pltpu.roll(v, shift, 0) for v in vals]
        vals = _bitonic_merge_desc([jnp.maximum(vals[i], partner[n - 1 - i]) for i in range(n)])
    return vals


def _top16_sorted(s_ref, hp, lanes):
    groups = [s_ref[hp, SUBLANES * i:SUBLANES * (i + 1), lanes]
              for i in range(PEER_NKEYS // SUBLANES)]
    return _merge_across_sublanes(_sort_desc(groups))


def _dup_bf16_bits(x):
    hi = pltpu.bitcast(x.astype(BF16).astype(F32), jnp.uint32)
    return hi | lax.shift_right_logical(hi, jnp.full(hi.shape, 16, jnp.uint32))


def _route_kernel(ss_ref, r1_ref, e1_ref, p2_ref, e2_ref):
    tokens = ss_ref.shape[-1]
    tf = PEER_ROUTE_TOKENS
    K = PEER_TOPK
    ngroups = PEER_NKEYS // SUBLANES
    sub = lax.broadcasted_iota(jnp.int32, (SUBLANES, tf), 0)

    def route_lanes(h, lanes):
        a = _top16_sorted(ss_ref, 2 * h, lanes)
        b = _top16_sorted(ss_ref, 2 * h + 1, lanes)
        a_lo = a[0]
        a_hi = a[SUBLANES]
        for i in range(1, SUBLANES):
            a_lo = jnp.where(sub == i, a[i], a_lo)
            a_hi = jnp.where(sub == i, a[SUBLANES + i], a_hi)
        cand = [a_lo + b[j] for j in range(K)]
        extra = a_hi + b[0]
        ins = [jnp.maximum(cand[0], extra)]
        for j in range(1, K):
            ins.append(jnp.maximum(cand[j], jnp.minimum(cand[j - 1], extra)))
        top = _merge_across_sublanes(ins)
        thr = top[K - 1]
        smax = top[0]
        z = jnp.exp(top[0] - smax)
        for j in range(1, K):
            z = z + jnp.exp(top[j] - smax)
        inv_z = 1.0 / z

        cap = []
        for j in range(K):
            cnt = (jnp.where(cand[j] >= thr, 1.0, 0.0)
                   + jnp.where(a_hi + b[j] >= thr, 1.0, 0.0))
            for shift in (4, 2, 1):
                cnt = cnt + pltpu.roll(cnt, shift, 0)
            cap.append(cnt)

        r1_rows, e1_rows, p2_rows, e2_rows = [], [], [], []
        for i in range(ngroups):
            s1 = ss_ref[2 * h, SUBLANES * i:SUBLANES * (i + 1), lanes]
            s2 = ss_ref[2 * h + 1, SUBLANES * i:SUBLANES * (i + 1), lanes]
            rank1 = jnp.full_like(s1, K + 1.0)
            cap2 = jnp.zeros_like(s2)
            for r in reversed(range(K)):
                rank1 = jnp.where(s1 == a[r], r + 1.0, rank1)
                cap2 = jnp.where(s2 == b[r], cap[r], cap2)
            r1_rows.append(rank1)
            p2_rows.append(cap2)
            e1_rows.append(jnp.exp(s1 - a[0]))
            e2_rows.append(jnp.exp(s2 - b[0]) * (0.5 * inv_z))
        r1_ref[0, h, :, lanes] = _dup_bf16_bits(jnp.concatenate(r1_rows, axis=0))
        e1_ref[0, h, :, lanes] = _dup_bf16_bits(jnp.concatenate(e1_rows, axis=0))
        p2_ref[0, h, :, lanes] = pltpu.bitcast(
            jnp.concatenate(p2_rows, axis=0).astype(BF16), jnp.uint32)
        e2_ref[0, h, :, lanes] = pltpu.bitcast(
            jnp.concatenate(e2_rows, axis=0).astype(BF16), jnp.uint32)

    def head_body(h, carry):
        for part in range(tokens // tf):
            route_lanes(h, slice(part * tf, (part + 1) * tf))
        for ref in (r1_ref, e1_ref, p2_ref, e2_ref):
            ref[0, h, :, tokens:] = jnp.zeros((ref.shape[2], ref.shape[3] - tokens), jnp.uint32)
        return carry

    lax.fori_loop(0, PEER_HEADS, head_body, 0)


def _route(ss, tg):
    nhp, nkeys, N = ss.shape
    pitch = tg + PEER_TABLE_PAD
    spec_in = pl.BlockSpec((nhp, nkeys, tg), lambda t: (0, 0, t))
    spec_out = pl.BlockSpec((1, PEER_HEADS, nkeys, pitch), lambda t: (t, 0, 0, 0))
    spec_packed = pl.BlockSpec((1, PEER_HEADS, nkeys // 2, pitch), lambda t: (t, 0, 0, 0))
    f32s = jax.ShapeDtypeStruct((N // tg, PEER_HEADS, nkeys, pitch), jnp.uint32)
    bf16s = jax.ShapeDtypeStruct((N // tg, PEER_HEADS, nkeys // 2, pitch), jnp.uint32)
    return pl.pallas_call(
        _route_kernel,
        grid=(N // tg,),
        in_specs=[spec_in],
        out_specs=[spec_out, spec_out, spec_packed, spec_packed],
        out_shape=[f32s, f32s, bf16s, bf16s],
        compiler_params=_params("arbitrary"),
        name="route",
    )(ss)


def _peer_kernel(h2t_ref, u_ref, vt_ref, r1_ref, e1_ref, p2_ref, e2_ref, x1_ref, mod_ref,
                 gpost_ref, o_ref, acc_ref, xa_ref, xb_ref, wa_ref, wb_ref):
    e = pl.program_id(2)
    n_e = pl.num_programs(2)
    n_sub = u_ref.shape[0]
    tg = h2t_ref.shape[1]
    rows_per_sub = PEER_SUB // PEER_NKEYS
    chunk = 2 * SUBLANES
    n_chunks = PEER_NKEYS // chunk
    c0 = math.sqrt(2.0 / math.pi)

    @pl.when(e == 0)
    def _():
        acc_ref[...] = jnp.zeros_like(acc_ref)

    xbufs = (xa_ref, xb_ref)
    wbufs = (wa_ref, wb_ref)

    tw = xa_ref.shape[1]
    n_split = tg // tw

    def unit(k):
        return k // n_split, slice((k % n_split) * tw, (k % n_split + 1) * tw)

    def scores_into(k):
        sb, cols = unit(k)
        xbufs[k % 2][...] = _dot(u_ref[sb], h2t_ref[:, cols]).astype(BF16)

    def stack(parts):
        return jnp.concatenate(parts, axis=0)

    def tile(ref, h, c, lanes):
        return pltpu.bitcast(ref[0, h, c * SUBLANES:(c + 1) * SUBLANES, lanes], BF16)

    def gates(k):
        sb, cols = unit(k)
        xbuf_ref = xbufs[k % 2]
        wbuf_ref = wbufs[k % 2]
        i1 = sb * rows_per_sub

        def row_bcast(ref, h, lanes):
            return stack([pltpu.bitcast(
                jnp.broadcast_to(ref[0, h, i1 + j:i1 + j + 1, lanes], (SUBLANES, 128)), BF16)
                for j in range(rows_per_sub)])

        zero = jnp.zeros((rows_per_sub * chunk, 128), BF16)
        for lg in range(tw // 128):
            lanes = slice(cols.start + lg * 128, cols.start + (lg + 1) * 128)
            local = slice(lg * 128, (lg + 1) * 128)
            g = [zero] * n_chunks
            for h in range(PEER_HEADS):
                rank1 = row_bcast(r1_ref, h, lanes)
                gate1 = row_bcast(e1_ref, h, lanes)
                for c in range(n_chunks):
                    cap2 = stack([tile(p2_ref, h, c, lanes)] * rows_per_sub)
                    gate2 = stack([tile(e2_ref, h, c, lanes)] * rows_per_sub)
                    g[c] = g[c] + jnp.where(rank1 <= cap2, gate2, 0.0) * gate1
            for c in range(n_chunks):
                x = stack([xbuf_ref[j * PEER_NKEYS + c * chunk:j * PEER_NKEYS + (c + 1) * chunk, local]
                           for j in range(rows_per_sub)])
                w = x * (1.0 + jnp.tanh(x * (c0 + (c0 * 0.044715) * (x * x)))) * g[c]
                for j in range(rows_per_sub):
                    wbuf_ref[j * PEER_NKEYS + c * chunk:j * PEER_NKEYS + (c + 1) * chunk, local] = (
                        w[j * chunk:(j + 1) * chunk])

    def accumulate(k):
        sb, cols = unit(k)
        acc_ref[:, cols] += _dot(vt_ref[sb], wbufs[k % 2][...])

    n_units = n_sub * n_split
    scores_into(0)
    for k in range(n_units):
        if k + 1 < n_units:
            scores_into(k + 1)
        if k >= 1:
            accumulate(k - 1)
        gates(k)
    accumulate(n_units - 1)

    @pl.when(e == n_e - 1)
    def _():
        y = acc_ref[...].T
        gate2 = mod_ref[0, 5:6, :]
        o_ref[...] = x1_ref[...] + gate2 * _rms(y, gpost_ref[...])


def _peer(h2t, u3, vt3, r1, e1, p2, e2, x1, mod3, g_post, B, S, tg, n_sub):
    D, N = h2t.shape
    n_sub_total = u3.shape[0]
    nblk = S // tg
    tok = lambda b, s, e: (b * nblk + s, 0)
    pitch = r1.shape[-1]
    tab_spec = pl.BlockSpec((1, PEER_HEADS, PEER_NKEYS // 2, pitch),
                            lambda b, s, e: (b * nblk + s, 0, 0, 0))
    rows_per_step = n_sub * PEER_SUB // PEER_NKEYS
    row_spec = pl.BlockSpec((1, PEER_HEADS, rows_per_step, pitch),
                            lambda b, s, e: (b * nblk + s, 0, e, 0))
    return pl.pallas_call(
        _peer_kernel,
        grid=(B, nblk, n_sub_total // n_sub),
        in_specs=[pl.BlockSpec((D, tg), lambda b, s, e: (0, b * nblk + s)),
                  pl.BlockSpec((n_sub, PEER_SUB, D), lambda b, s, e: (e, 0, 0)),
                  pl.BlockSpec((n_sub, D, PEER_SUB), lambda b, s, e: (e, 0, 0)),
                  row_spec, row_spec, tab_spec, tab_spec,
                  pl.BlockSpec((tg, D), tok),
                  pl.BlockSpec((1, 6, D), lambda b, s, e: (b, 0, 0)),
                  pl.BlockSpec((1, D), lambda b, s, e: (0, 0))],
        out_specs=pl.BlockSpec((tg, D), tok),
        out_shape=jax.ShapeDtypeStruct((N, D), F32),
        scratch_shapes=[pltpu.VMEM((D, tg), F32),
                        pltpu.VMEM((PEER_SUB, PEER_UNIT_TOKENS), BF16),
                        pltpu.VMEM((PEER_SUB, PEER_UNIT_TOKENS), BF16),
                        pltpu.VMEM((PEER_SUB, PEER_UNIT_TOKENS), BF16),
                        pltpu.VMEM((PEER_SUB, PEER_UNIT_TOKENS), BF16)],
        compiler_params=_params("arbitrary", "arbitrary", "arbitrary"),
        name="peer",
    )(h2t, u3, vt3, r1, e1, p2, e2, x1, mod3, g_post)


def _block(n, cap):
    b = min(n, cap)
    assert n % b == 0, (n, b)
    return b


def _layer(x, c, w_ada, b_ada, norm_pre_mix, norm_post_mix, w_in, w_alpha, b_alpha, gla_norm,
           conv_w, conv_b, lru_w_a, lru_b_a, lru_w_x, lru_b_x, lru_lambda, w_out,
           norm_pre_ffn, norm_post_ffn, peer_w_query, peer_sub_keys, peer_u, peer_v):
    B, S, D = x.shape
    N = B * S
    dk = w_alpha.shape[1]
    rank = w_alpha.shape[0]
    assert S % GLA_CHUNK == 0 and D % 128 == 0

    o_q, o_k, o_v, o_r, o_al, o_xl, o_xg, o_ga, o_gb = (
        0, dk, 2 * dk, 2 * dk + D, 2 * dk + 2 * D, 2 * dk + 2 * D + rank,
        2 * dk + 3 * D + rank, 2 * dk + 4 * D + rank, 2 * dk + 5 * D + rank)
    w_main = jnp.concatenate(
        [w_in[:, o_q:o_r], w_in[:, o_r:o_al], w_in[:, o_xl:]], axis=1).astype(BF16)
    w_al = jnp.pad(w_in[:, o_al:o_xl], ((0, 0), (0, GLA_RANK_PAD - rank))).astype(BF16)
    w_alpha_p = jnp.pad(w_alpha, ((0, GLA_RANK_PAD - rank), (0, 0))).astype(BF16)
    row = lambda a: a.reshape(1, -1).astype(F32)

    mod3 = _ada(c.astype(F32), w_ada.astype(F32), b_ada.astype(F32)).reshape(B, 6, D)
    x2 = x.reshape(N, D)

    tb = _block(S, 512)
    qk, v, r, xl, xg, ga, gb, la = _inproj(
        x2, mod3, row(norm_pre_mix), w_main, w_al, w_alpha_p, row(b_alpha), B, S, tb)

    y_gla = _gla(qk, v, r, la, row(gla_norm), B, S, _block(S, 256))
    y_lru = _lru(xl, xg, conv_w.astype(F32), row(conv_b), lru_w_a.astype(BF16), row(lru_b_a),
                 lru_w_x.astype(BF16), row(lru_b_x), row(lru_lambda), B, S, _block(S, 32))

    nh, _, nkeys, half = peer_sub_keys.shape
    keys = peer_sub_keys.reshape(nh * 2, nkeys, half).astype(BF16)
    x1, h2t, ss = _mix(y_gla, y_lru, ga, gb, x2, mod3, w_out.astype(BF16), row(norm_post_mix),
                      row(norm_pre_ffn), peer_w_query.astype(BF16), keys, B, S, _block(S, 512))

    tg = _block(S, 512)
    r1, e1, p2, e2 = _route(ss, tg)

    n_exp = peer_u.shape[0]
    u3 = peer_u.astype(BF16).reshape(n_exp // PEER_SUB, PEER_SUB, D)
    vt3 = peer_v.astype(BF16).reshape(n_exp // PEER_SUB, PEER_SUB, D).transpose(0, 2, 1)
    out = _peer(h2t, u3, vt3, r1, e1, p2, e2, x1, mod3, row(norm_post_ffn), B, S, tg,
                PEER_STEP_EXPERTS // PEER_SUB)
    return out.reshape(B, S, D)


def kernel(x, c, w_ada, b_ada, norm_pre_mix, norm_post_mix, w_in, w_alpha, b_alpha, gla_norm, conv_w, conv_b, lru_w_a, lru_b_a, lru_w_x, lru_b_x, lru_lambda, w_out, norm_pre_ffn, norm_post_ffn, peer_w_query, peer_sub_keys, peer_u, peer_v):
    depth = w_ada.shape[0]
    for l in range(depth):
        x = _layer(x, c, w_ada[l], b_ada[l], norm_pre_mix[l], norm_post_mix[l], w_in[l],
                   w_alpha[l], b_alpha[l], gla_norm[l], conv_w[l], conv_b[l], lru_w_a[l],
                   lru_b_a[l], lru_w_x[l], lru_b_x[l], lru_lambda[l], w_out[l],
                   norm_pre_ffn[l], norm_post_ffn[l], peer_w_query[l], peer_sub_keys[l],
                   peer_u[l], peer_v[l])
    return x
```

```python
import math

import jax
import jax.numpy as jnp
from jax import lax
from jax.experimental import pallas as pl
from jax.experimental.pallas import tpu as pltpu

F32 = jnp.float32
BF16 = jnp.bfloat16

EPS = 1e-6
GLA_HEADS = 4
GLA_RANK_PAD = 128
GLA_TAU = 16.0
GLA_CHUNK = 128
LRU_BLOCKS = 8
CONV_WIDTH = 4
LRU_C = 8.0
PEER_HEADS = 8
PEER_NKEYS = 128
PEER_TOPK = 16
PEER_SUB = 512
PEER_STEP_EXPERTS = 2048
PEER_UNIT_TOKENS = 512
PEER_ROUTE_TOKENS = 256
PEER_TABLE_PAD = 128
SUBLANES = 8

VMEM_LIMIT = 56 * 1024 * 1024

NT_DIMS = (((1,), (1,)), ((), ()))
TN_DIMS = (((0,), (0,)), ((), ()))


def _dot(a, b):
    return jnp.dot(a, b, preferred_element_type=F32)


def _dot_nt(a, b):
    return lax.dot_general(a, b, NT_DIMS, preferred_element_type=F32)


def _dot_tn(a, b):
    return lax.dot_general(a, b, TN_DIMS, preferred_element_type=F32)


def _sigmoid(x):
    return 1.0 / (1.0 + jnp.exp(-x))


def _gelu_tanh(x):
    c0 = math.sqrt(2.0 / math.pi)
    return 0.5 * x * (1.0 + jnp.tanh(c0 * (x + 0.044715 * (x * x * x))))


def _rms(x, gain):
    ms = jnp.mean(x * x, axis=-1, keepdims=True)
    return x * lax.rsqrt(ms + EPS) * gain


def _params(*sem, flags=None):
    return pltpu.CompilerParams(dimension_semantics=sem, vmem_limit_bytes=VMEM_LIMIT, flags=flags)


def _ada_kernel(c_ref, w_ref, b_ref, o_ref):
    c = c_ref[...]
    s = c * _sigmoid(c)
    o_ref[...] = jnp.dot(s, w_ref[...], precision=lax.Precision.HIGHEST,
                         preferred_element_type=F32) + b_ref[...]


def _ada(c, w_ada, b_ada):
    B, D = c.shape
    n_out = w_ada.shape[1]
    blk = 1024
    return pl.pallas_call(
        _ada_kernel,
        grid=(n_out // blk,),
        in_specs=[pl.BlockSpec((B, D), lambda j: (0, 0)),
                  pl.BlockSpec((D, blk), lambda j: (0, j)),
                  pl.BlockSpec((1, blk), lambda j: (0, j))],
        out_specs=pl.BlockSpec((B, blk), lambda j: (0, j)),
        out_shape=jax.ShapeDtypeStruct((B, n_out), F32),
        compiler_params=_params("arbitrary"),
        name="ada",
    )(c, w_ada, b_ada.reshape(1, n_out))


def _inproj_kernel(x_ref, mod_ref, gain_ref, w_ref, wal_ref, walpha_ref, balpha_ref,
                   qk_ref, v_ref, r_ref, xl_ref, xg_ref, ga_ref, gb_ref, la_ref):
    D = x_ref.shape[-1]
    x = x_ref[...]
    shift = mod_ref[0, 0:1, :]
    scale = mod_ref[0, 1:2, :]
    h = (_rms(x, gain_ref[...]) * (1.0 + scale) + shift).astype(BF16)
    a_low = _dot(h, wal_ref[...]).astype(BF16)
    z = _dot(a_low, walpha_ref[...]) + balpha_ref[...]
    log_sig = jnp.minimum(z, 0.0) - jnp.log(1.0 + jnp.exp(-jnp.abs(z)))
    la_ref[...] = log_sig * (1.0 / GLA_TAU)
    for j, o_ref in enumerate((qk_ref, v_ref, r_ref, xl_ref, xg_ref, ga_ref, gb_ref)):
        o_ref[...] = _dot(h, w_ref[:, j * D:(j + 1) * D]).astype(BF16)


def _inproj(x2, mod3, gain, w_main, w_al, w_alpha, b_alpha, B, S, tb):
    N, D = x2.shape
    dk = w_alpha.shape[1]
    nblk = S // tb
    tok = lambda b, s: (b * nblk + s, 0)
    const = lambda b, s: (0, 0)
    big = jax.ShapeDtypeStruct((N, D), BF16)
    return pl.pallas_call(
        _inproj_kernel,
        grid=(B, nblk),
        in_specs=[pl.BlockSpec((tb, D), tok),
                  pl.BlockSpec((1, 6, D), lambda b, s: (b, 0, 0)),
                  pl.BlockSpec((1, D), const),
                  pl.BlockSpec(w_main.shape, const, pipeline_mode=pl.Buffered(1)),
                  pl.BlockSpec(w_al.shape, const),
                  pl.BlockSpec(w_alpha.shape, const),
                  pl.BlockSpec((1, dk), const)],
        out_specs=[pl.BlockSpec((tb, D), tok)] * 7 + [pl.BlockSpec((tb, dk), tok)],
        out_shape=[big] * 7 + [jax.ShapeDtypeStruct((N, dk), F32)],
        compiler_params=_params("arbitrary", "arbitrary"),
        name="inproj",
    )(x2, mod3, gain, w_main, w_al, w_alpha, b_alpha)


def _gla_kernel(qk_ref, v_ref, r_ref, la_ref, gn_ref, tri_ref, o_ref, st_ref):
    tc = qk_ref.shape[0]
    dk_all = la_ref.shape[-1]
    hdk = dk_all // GLA_HEADS
    hdv = v_ref.shape[-1] // GLA_HEADS
    C = GLA_CHUNK
    n_c = tc // C
    mid = C // 2 - 1

    @pl.when(pl.program_id(1) == 0)
    def _():
        st_ref[...] = jnp.zeros_like(st_ref)

    la = la_ref[...]
    p0 = la.astype(BF16)
    rem = la - p0.astype(F32)
    p1 = rem.astype(BF16)
    p2 = (rem - p1.astype(F32)).astype(BF16)
    tri = tri_ref[...]
    cum = _dot(tri, p0) + _dot(tri, p1) + _dot(tri, p2)

    def per_chunk_row(r):
        return jnp.concatenate(
            [jnp.broadcast_to(cum[c * C + r:c * C + r + 1], (C, dk_all)) for c in range(n_c)], axis=0)

    cm = per_chunk_row(mid)
    cl = per_chunk_row(C - 1)
    qt = qk_ref[:, 0:dk_all].astype(F32) * (hdk ** -0.5) * jnp.exp(cum - cm)
    kt = qk_ref[:, dk_all:2 * dk_all].astype(F32) * jnp.exp(cm - cum)
    qi = (qt * jnp.exp(cm)).astype(BF16)
    kd = (kt * jnp.exp(cl - cm)).astype(BF16)
    qt = qt.astype(BF16)
    kt = kt.astype(BF16)

    row = lax.broadcasted_iota(jnp.int32, (C, C), 0)
    col = lax.broadcasted_iota(jnp.int32, (C, C), 1)
    causal = row >= col
    pairs = [(c, h) for c in range(n_c) for h in range(GLA_HEADS)]
    rows = lambda c: slice(c * C, (c + 1) * C)
    ks = lambda h: slice(h * hdk, (h + 1) * hdk)
    vs = lambda h: slice(h * hdv, (h + 1) * hdv)

    scores = {p: jnp.where(causal, _dot_nt(qt[rows(p[0]), ks(p[1])], kt[rows(p[0]), ks(p[1])]),
                           0.0).astype(BF16) for p in pairs}
    intra = {p: _dot(scores[p], v_ref[rows(p[0]), vs(p[1])]) for p in pairs}
    update = {p: _dot_tn(v_ref[rows(p[0]), vs(p[1])], kd[rows(p[0]), ks(p[1])]) for p in pairs}
    state_in = {}
    for h in range(GLA_HEADS):
        st = st_ref[h]
        for c in range(n_c):
            state_in[(c, h)] = st.astype(BF16)
            e_last = jnp.exp(cum[(c + 1) * C - 1:(c + 1) * C, ks(h)])
            st = st * e_last + update[(c, h)]
        st_ref[h] = st
    for c, h in pairs:
        o = intra[(c, h)] + _dot_nt(qi[rows(c), ks(h)], state_in[(c, h)])
        y = _rms(o, gn_ref[:, vs(h)])
        rr = r_ref[rows(c), vs(h)].astype(F32)
        o_ref[rows(c), vs(h)] = (y * (rr * _sigmoid(rr))).astype(BF16)


def _gla(qk, v, r, la, gla_norm, B, S, tc):
    N, D = v.shape
    dk = la.shape[1]
    nblk = S // tc
    tok = lambda b, s: (b * nblk + s, 0)
    const = lambda b, s: (0, 0)
    idx = jnp.arange(tc)
    tri = ((idx[:, None] >= idx[None, :]) &
           (idx[:, None] // GLA_CHUNK == idx[None, :] // GLA_CHUNK)).astype(BF16)
    return pl.pallas_call(
        _gla_kernel,
        grid=(B, nblk),
        in_specs=[pl.BlockSpec((tc, 2 * dk), tok),
                  pl.BlockSpec((tc, D), tok),
                  pl.BlockSpec((tc, D), tok),
                  pl.BlockSpec((tc, dk), tok),
                  pl.BlockSpec((1, D), const),
                  pl.BlockSpec((tc, tc), const)],
        out_specs=pl.BlockSpec((tc, D), tok),
        out_shape=jax.ShapeDtypeStruct((N, D), BF16),
        scratch_shapes=[pltpu.VMEM((GLA_HEADS, D // GLA_HEADS, dk // GLA_HEADS), F32)],
        compiler_params=_params("arbitrary", "arbitrary"),
        name="gla",
    )(qk, v, r, la, gla_norm, tri)


def _lru_kernel(xl_ref, xg_ref, perm_ref, cw_ref, cb_ref, wa_ref, ba_ref, wx_ref, bx_ref,
                lam_ref, o_ref, xbuf_ref, h_ref):
    nb, tt, W = xl_ref.shape
    rows = nb * tt
    bw = W // LRU_BLOCKS
    tail = (CONV_WIDTH - 1) * nb

    @pl.when(pl.program_id(0) == 0)
    def _():
        xbuf_ref[0:tail, :] = jnp.zeros((tail, W), F32)
        h_ref[...] = jnp.zeros_like(h_ref)

    perm = perm_ref[...]
    xl = _dot(perm, xl_ref[...].reshape(rows, W))
    xg = _dot(perm, xg_ref[...].reshape(rows, W))

    xbuf_ref[tail:tail + rows, :] = xl
    xc = cb_ref[...]
    for j in range(CONV_WIDTH):
        xc = xc + cw_ref[j:j + 1, :] * xbuf_ref[j * nb:j * nb + rows, :]
    xbuf_ref[0:tail, :] = xbuf_ref[rows:rows + tail, :]

    xcb = xc.astype(BF16)
    gr_parts, gi_parts = [], []
    for n in range(LRU_BLOCKS):
        blk = xcb[:, n * bw:(n + 1) * bw]
        gr_parts.append(_dot(blk, wa_ref[n]))
        gi_parts.append(_dot(blk, wx_ref[n]))
    gate_r = _sigmoid(jnp.concatenate(gr_parts, axis=1) + ba_ref[...])
    gate_i = _sigmoid(jnp.concatenate(gi_parts, axis=1) + bx_ref[...])
    neg_lam = -lam_ref[...]
    softplus = jnp.maximum(neg_lam, 0.0) + jnp.log(1.0 + jnp.exp(-jnp.abs(neg_lam)))
    log_a = (-LRU_C) * gate_r * softplus
    a = jnp.exp(log_a)
    u = jnp.sqrt(jnp.maximum(1.0 - jnp.exp(2.0 * log_a), 0.0)) * (gate_i * xc)

    h = h_ref[...]
    steps = []
    for t in range(tt):
        grp = slice(t * nb, (t + 1) * nb)
        h = a[grp] * h + u[grp]
        steps.append(h)
    h_ref[...] = h
    y = (jnp.concatenate(steps, axis=0) * _gelu_tanh(xg)).astype(BF16)
    y = lax.dot_general(perm, y, TN_DIMS, preferred_element_type=F32).astype(BF16)
    o_ref[...] = y.reshape(nb, tt, W)


def _lru(xl, xg, conv_w, conv_b, wa, ba, wx, bx, lam, B, S, tt):
    N, W = xl.shape
    assert B == SUBLANES, "the time-major layout places the batch on the sublane axis"
    rows = B * tt
    r = jnp.arange(rows)
    perm = (r[None, :] == (r[:, None] % B) * tt + r[:, None] // B).astype(BF16)
    blk = lambda s: (0, s, 0)
    const2 = lambda s: (0, 0)
    const3 = lambda s: (0, 0, 0)
    out = pl.pallas_call(
        _lru_kernel,
        grid=(S // tt,),
        in_specs=[pl.BlockSpec((B, tt, W), blk),
                  pl.BlockSpec((B, tt, W), blk),
                  pl.BlockSpec((rows, rows), const2),
                  pl.BlockSpec(conv_w.shape, const2),
                  pl.BlockSpec((1, W), const2),
                  pl.BlockSpec(wa.shape, const3),
                  pl.BlockSpec((1, W), const2),
                  pl.BlockSpec(wx.shape, const3),
                  pl.BlockSpec((1, W), const2),
                  pl.BlockSpec((1, W), const2)],
        out_specs=pl.BlockSpec((B, tt, W), blk),
        out_shape=jax.ShapeDtypeStruct((B, S, W), BF16),
        scratch_shapes=[pltpu.VMEM((rows + (CONV_WIDTH - 1) * B, W), F32),
                        pltpu.VMEM((B, W), F32)],
        compiler_params=_params("arbitrary"),
        name="lru",
    )(xl.reshape(B, S, W), xg.reshape(B, S, W), perm, conv_w, conv_b, wa, ba, wx, bx, lam)
    return out.reshape(N, W)


def _mix_kernel(yg_ref, yl_ref, ga_ref, gb_ref, x_ref, mod_ref, wout_ref, gpost_ref,
                gpre_ref, wq_ref, keys_ref, x1_ref, h2t_ref, ss_ref):
    merged = _sigmoid(ga_ref[...]) * yg_ref[...] + _sigmoid(gb_ref[...]) * yl_ref[...]
    y = _dot(merged, wout_ref[...])
    gate1 = mod_ref[0, 2:3, :]
    x1 = x_ref[...] + gate1 * _rms(y, gpost_ref[...])
    x1_ref[...] = x1
    shift2 = mod_ref[0, 3:4, :]
    scale2 = mod_ref[0, 4:5, :]
    h2f = _rms(x1, gpre_ref[...]) * (1.0 + scale2) + shift2
    h2 = h2f.astype(BF16)
    h2t_ref[...] = h2f.T.astype(BF16)
    qry = _dot(h2, wq_ref[...]).astype(BF16)
    half = keys_ref.shape[-1]
    for hp in range(keys_ref.shape[0]):
        ss_ref[hp] = _dot_nt(keys_ref[hp], qry[:, hp * half:(hp + 1) * half])


def _mix(yg, yl, ga, gb, x2, mod3, w_out, g_post, g_pre, w_q, keys, B, S, te):
    N, D = x2.shape
    nblk = S // te
    tok = lambda b, s: (b * nblk + s, 0)
    const2 = lambda b, s: (0, 0)
    nhp, nkeys, _ = keys.shape
    return pl.pallas_call(
        _mix_kernel,
        grid=(B, nblk),
        in_specs=[pl.BlockSpec((te, D), tok)] * 5 + [
            pl.BlockSpec((1, 6, D), lambda b, s: (b, 0, 0)),
            pl.BlockSpec(w_out.shape, const2),
            pl.BlockSpec((1, D), const2),
            pl.BlockSpec((1, D), const2),
            pl.BlockSpec(w_q.shape, const2),
            pl.BlockSpec(keys.shape, lambda b, s: (0, 0, 0))],
        out_specs=[pl.BlockSpec((te, D), tok),
                   pl.BlockSpec((D, te), lambda b, s: (0, b * nblk + s)),
                   pl.BlockSpec((nhp, nkeys, te), lambda b, s: (0, 0, b * nblk + s))],
        out_shape=[jax.ShapeDtypeStruct((N, D), F32),
                   jax.ShapeDtypeStruct((D, N), BF16),
                   jax.ShapeDtypeStruct((nhp, nkeys, N), F32)],
        compiler_params=_params("arbitrary", "arbitrary"),
        name="mix",
    )(yg, yl, ga, gb, x2, mod3, w_out, g_post, g_pre, w_q, keys)


def _oddeven_merge_sort_pairs(n):
    pairs = []
    p = 1
    while p < n:
        k = p
        while k >= 1:
            for j in range(k % p, n - k, 2 * k):
                for i in range(min(k, n - j - k)):
                    if (i + j) // (2 * p) == (i + j + k) // (2 * p):
                        pairs.append((i + j, i + j + k))
            k //= 2
        p *= 2
    return pairs


_SORT16 = _oddeven_merge_sort_pairs(PEER_TOPK)


def _sort_desc(vals):
    vals = list(vals)
    for i, j in _SORT16:
        hi = jnp.maximum(vals[i], vals[j])
        lo = jnp.minimum(vals[i], vals[j])
        vals[i], vals[j] = hi, lo
    return vals


def _bitonic_merge_desc(vals):
    vals = list(vals)
    n = len(vals)
    d = n // 2
    while d >= 1:
        for i in range(n):
            if (i & d) == 0:
                hi = jnp.maximum(vals[i], vals[i + d])
                lo = jnp.minimum(vals[i], vals[i + d])
                vals[i], vals[i + d] = hi, lo
        d //= 2
    return vals


def _merge_across_sublanes(vals):
    n = len(vals)
    for shift in (4, 2, 1):
        partner = [pltpu.roll(v, shift, 0) for v in vals]
        vals = _bitonic_merge_desc([jnp.maximum(vals[i], partner[n - 1 - i]) for i in range(n)])
    return vals


def _top16_sorted(s_ref, hp, lanes):
    groups = [s_ref[hp, SUBLANES * i:SUBLANES * (i + 1), lanes]
              for i in range(PEER_NKEYS // SUBLANES)]
    return _merge_across_sublanes(_sort_desc(groups))


def _dup_bf16_bits(x):
    hi = pltpu.bitcast(x.astype(BF16).astype(F32), jnp.uint32)
    return hi | lax.shift_right_logical(hi, jnp.full(hi.shape, 16, jnp.uint32))


def _route_kernel(ss_ref, r1_ref, e1_ref, p2_ref, e2_ref):
    tokens = ss_ref.shape[-1]
    tf = PEER_ROUTE_TOKENS
    K = PEER_TOPK
    ngroups = PEER_NKEYS // SUBLANES
    sub = lax.broadcasted_iota(jnp.int32, (SUBLANES, tf), 0)

    def route_lanes(h, lanes):
        a = _top16_sorted(ss_ref, 2 * h, lanes)
        b = _top16_sorted(ss_ref, 2 * h + 1, lanes)
        a_lo = a[0]
        a_hi = a[SUBLANES]
        for i in range(1, SUBLANES):
            a_lo = jnp.where(sub == i, a[i], a_lo)
            a_hi = jnp.where(sub == i, a[SUBLANES + i], a_hi)
        cand = [a_lo + b[j] for j in range(K)]
        extra = a_hi + b[0]
        ins = [jnp.maximum(cand[0], extra)]
        for j in range(1, K):
            ins.append(jnp.maximum(cand[j], jnp.minimum(cand[j - 1], extra)))
        top = _merge_across_sublanes(ins)
        thr = top[K - 1]
        smax = top[0]
        z = jnp.exp(top[0] - smax)
        for j in range(1, K):
            z = z + jnp.exp(top[j] - smax)
        inv_z = 1.0 / z

        cap = []
        for j in range(K):
            cnt = (jnp.where(cand[j] >= thr, 1.0, 0.0)
                   + jnp.where(a_hi + b[j] >= thr, 1.0, 0.0))
            for shift in (4, 2, 1):
                cnt = cnt + pltpu.roll(cnt, shift, 0)
            cap.append(cnt)

        r1_rows, e1_rows, p2_rows, e2_rows = [], [], [], []
        for i in range(ngroups):
            s1 = ss_ref[2 * h, SUBLANES * i:SUBLANES * (i + 1), lanes]
            s2 = ss_ref[2 * h + 1, SUBLANES * i:SUBLANES * (i + 1), lanes]
            rank1 = jnp.full_like(s1, K + 1.0)
            cap2 = jnp.zeros_like(s2)
            for r in reversed(range(K)):
                rank1 = jnp.where(s1 == a[r], r + 1.0, rank1)
                cap2 = jnp.where(s2 == b[r], cap[r], cap2)
            r1_rows.append(rank1)
            p2_rows.append(cap2)
            e1_rows.append(jnp.exp(s1 - a[0]))
            e2_rows.append(jnp.exp(s2 - b[0]) * (0.5 * inv_z))
        r1_ref[0, h, :, lanes] = _dup_bf16_bits(jnp.concatenate(r1_rows, axis=0))
        e1_ref[0, h, :, lanes] = _dup_bf16_bits(jnp.concatenate(e1_rows, axis=0))
        p2_ref[0, h, :, lanes] = pltpu.bitcast(
            jnp.concatenate(p2_rows, axis=0).astype(BF16), jnp.uint32)
        e2_ref[0, h, :, lanes] = pltpu.bitcast(
            jnp.concatenate(e2_rows, axis=0).astype(BF16), jnp.uint32)

    def head_body(h, carry):
        for part in range(tokens // tf):
            route_lanes(h, slice(part * tf, (part + 1) * tf))
        for ref in (r1_ref, e1_ref, p2_ref, e2_ref):
            ref[0, h, :, tokens:] = jnp.zeros((ref.shape[2], ref.shape[3] - tokens), jnp.uint32)
        return carry

    lax.fori_loop(0, PEER_HEADS, head_body, 0)


def _route(ss, tg):
    nhp, nkeys, N = ss.shape
    pitch = tg + PEER_TABLE_PAD
    spec_in = pl.BlockSpec((nhp, nkeys, tg), lambda t: (0, 0, t))
    spec_out = pl.BlockSpec((1, PEER_HEADS, nkeys, pitch), lambda t: (t, 0, 0, 0))
    spec_packed = pl.BlockSpec((1, PEER_HEADS, nkeys // 2, pitch), lambda t: (t, 0, 0, 0))
    f32s = jax.ShapeDtypeStruct((N // tg, PEER_HEADS, nkeys, pitch), jnp.uint32)
    bf16s = jax.ShapeDtypeStruct((N // tg, PEER_HEADS, nkeys // 2, pitch), jnp.uint32)
    return pl.pallas_call(
        _route_kernel,
        grid=(N // tg,),
        in_specs=[spec_in],
        out_specs=[spec_out, spec_out, spec_packed, spec_packed],
        out_shape=[f32s, f32s, bf16s, bf16s],
        compiler_params=_params("arbitrary"),
        name="route",
    )(ss)


def _peer_kernel(h2t_ref, u_ref, vt_ref, r1_ref, e1_ref, p2_ref, e2_ref, x1_ref, mod_ref,
                 gpost_ref, o_ref, acc_ref, xa_ref, xb_ref, wa_ref, wb_ref):
    e = pl.program_id(2)
    n_e = pl.num_programs(2)
    n_sub = u_ref.shape[0]
    tg = h2t_ref.shape[1]
    rows_per_sub = PEER_SUB // PEER_NKEYS
    chunk = 2 * SUBLANES
    n_chunks = PEER_NKEYS // chunk
    c0 = math.sqrt(2.0 / math.pi)

    @pl.when(e == 0)
    def _():
        acc_ref[...] = jnp.zeros_like(acc_ref)

    xbufs = (xa_ref, xb_ref)
    wbufs = (wa_ref, wb_ref)

    tw = xa_ref.shape[1]
    n_split = tg // tw

    def unit(k):
        return k // n_split, slice((k % n_split) * tw, (k % n_split + 1) * tw)

    def scores_into(k):
        sb, cols = unit(k)
        xbufs[k % 2][...] = _dot(u_ref[sb], h2t_ref[:, cols]).astype(BF16)

    def stack(parts):
        return jnp.concatenate(parts, axis=0)

    def tile(ref, h, c, lanes):
        return pltpu.bitcast(ref[0, h, c * SUBLANES:(c + 1) * SUBLANES, lanes], BF16)

    def gates(k):
        sb, cols = unit(k)
        xbuf_ref = xbufs[k % 2]
        wbuf_ref = wbufs[k % 2]
        i1 = sb * rows_per_sub

        def row_bcast(ref, h, lanes):
            return stack([pltpu.bitcast(
                jnp.broadcast_to(ref[0, h, i1 + j:i1 + j + 1, lanes], (SUBLANES, 128)), BF16)
                for j in range(rows_per_sub)])

        zero = jnp.zeros((rows_per_sub * chunk, 128), BF16)
        for lg in range(tw // 128):
            lanes = slice(cols.start + lg * 128, cols.start + (lg + 1) * 128)
            local = slice(lg * 128, (lg + 1) * 128)
            g = [zero] * n_chunks
            for h in range(PEER_HEADS):
                rank1 = row_bcast(r1_ref, h, lanes)
                gate1 = row_bcast(e1_ref, h, lanes)
                for c in range(n_chunks):
                    cap2 = stack([tile(p2_ref, h, c, lanes)] * rows_per_sub)
                    gate2 = stack([tile(e2_ref, h, c, lanes)] * rows_per_sub)
                    g[c] = g[c] + jnp.where(rank1 <= cap2, gate2, 0.0) * gate1
            for c in range(n_chunks):
                x = stack([xbuf_ref[j * PEER_NKEYS + c * chunk:j * PEER_NKEYS + (c + 1) * chunk, local]
                           for j in range(rows_per_sub)])
                w = x * (1.0 + jnp.tanh(x * (c0 + (c0 * 0.044715) * (x * x)))) * g[c]
                for j in range(rows_per_sub):
                    wbuf_ref[j * PEER_NKEYS + c * chunk:j * PEER_NKEYS + (c + 1) * chunk, local] = (
                        w[j * chunk:(j + 1) * chunk])

    def accumulate(k):
        sb, cols = unit(k)
        acc_ref[:, cols] += _dot(vt_ref[sb], wbufs[k % 2][...])

    n_units = n_sub * n_split
    scores_into(0)
    for k in range(n_units):
        if k + 1 < n_units:
            scores_into(k + 1)
        if k >= 1:
            accumulate(k - 1)
        gates(k)
    accumulate(n_units - 1)

    @pl.when(e == n_e - 1)
    def _():
        y = acc_ref[...].T
        gate2 = mod_ref[0, 5:6, :]
        o_ref[...] = x1_ref[...] + gate2 * _rms(y, gpost_ref[...])


def _peer(h2t, u3, vt3, r1, e1, p2, e2, x1, mod3, g_post, B, S, tg, n_sub):
    D, N = h2t.shape
    n_sub_total = u3.shape[0]
    nblk = S // tg
    tok = lambda b, s, e: (b * nblk + s, 0)
    pitch = r1.shape[-1]
    tab_spec = pl.BlockSpec((1, PEER_HEADS, PEER_NKEYS // 2, pitch),
                            lambda b, s, e: (b * nblk + s, 0, 0, 0))
    rows_per_step = n_sub * PEER_SUB // PEER_NKEYS
    row_spec = pl.BlockSpec((1, PEER_HEADS, rows_per_step, pitch),
                            lambda b, s, e: (b * nblk + s, 0, e, 0))
    return pl.pallas_call(
        _peer_kernel,
        grid=(B, nblk, n_sub_total // n_sub),
        in_specs=[pl.BlockSpec((D, tg), lambda b, s, e: (0, b * nblk + s)),
                  pl.BlockSpec((n_sub, PEER_SUB, D), lambda b, s, e: (e, 0, 0)),
                  pl.BlockSpec((n_sub, D, PEER_SUB), lambda b, s, e: (e, 0, 0)),
                  row_spec, row_spec, tab_spec, tab_spec,
                  pl.BlockSpec((tg, D), tok),
                  pl.BlockSpec((1, 6, D), lambda b, s, e: (b, 0, 0)),
                  pl.BlockSpec((1, D), lambda b, s, e: (0, 0))],
        out_specs=pl.BlockSpec((tg, D), tok),
        out_shape=jax.ShapeDtypeStruct((N, D), F32),
        scratch_shapes=[pltpu.VMEM((D, tg), F32),
                        pltpu.VMEM((PEER_SUB, PEER_UNIT_TOKENS), BF16),
                        pltpu.VMEM((PEER_SUB, PEER_UNIT_TOKENS), BF16),
                        pltpu.VMEM((PEER_SUB, PEER_UNIT_TOKENS), BF16),
                        pltpu.VMEM((PEER_SUB, PEER_UNIT_TOKENS), BF16)],
        compiler_params=_params("arbitrary", "arbitrary", "arbitrary"),
        name="peer",
    )(h2t, u3, vt3, r1, e1, p2, e2, x1, mod3, g_post)


def _block(n, cap):
    b = min(n, cap)
    assert n % b == 0, (n, b)
    return b


def _layer(x, c, w_ada, b_ada, norm_pre_mix, norm_post_mix, w_in, w_alpha, b_alpha, gla_norm,
           conv_w, conv_b, lru_w_a, lru_b_a, lru_w_x, lru_b_x, lru_lambda, w_out,
           norm_pre_ffn, norm_post_ffn, peer_w_query, peer_sub_keys, peer_u, peer_v):
    B, S, D = x.shape
    N = B * S
    dk = w_alpha.shape[1]
    rank = w_alpha.shape[0]
    assert S % GLA_CHUNK == 0 and D % 128 == 0

    o_q, o_k, o_v, o_r, o_al, o_xl, o_xg, o_ga, o_gb = (
        0, dk, 2 * dk, 2 * dk + D, 2 * dk + 2 * D, 2 * dk + 2 * D + rank,
        2 * dk + 3 * D + rank, 2 * dk + 4 * D + rank, 2 * dk + 5 * D + rank)
    w_main = jnp.concatenate(
        [w_in[:, o_q:o_r], w_in[:, o_r:o_al], w_in[:, o_xl:]], axis=1).astype(BF16)
    w_al = jnp.pad(w_in[:, o_al:o_xl], ((0, 0), (0, GLA_RANK_PAD - rank))).astype(BF16)
    w_alpha_p = jnp.pad(w_alpha, ((0, GLA_RANK_PAD - rank), (0, 0))).astype(BF16)
    row = lambda a: a.reshape(1, -1).astype(F32)

    mod3 = _ada(c.astype(F32), w_ada.astype(F32), b_ada.astype(F32)).reshape(B, 6, D)
    x2 = x.reshape(N, D)

    tb = _block(S, 512)
    qk, v, r, xl, xg, ga, gb, la = _inproj(
        x2, mod3, row(norm_pre_mix), w_main, w_al, w_alpha_p, row(b_alpha), B, S, tb)

    y_gla = _gla(qk, v, r, la, row(gla_norm), B, S, _block(S, 512))
    y_lru = _lru(xl, xg, conv_w.astype(F32), row(conv_b), lru_w_a.astype(BF16), row(lru_b_a),
                 lru_w_x.astype(BF16), row(lru_b_x), row(lru_lambda), B, S, _block(S, 64))

    nh, _, nkeys, half = peer_sub_keys.shape
    keys = peer_sub_keys.reshape(nh * 2, nkeys, half).astype(BF16)
    x1, h2t, ss = _mix(y_gla, y_lru, ga, gb, x2, mod3, w_out.astype(BF16), row(norm_post_mix),
                      row(norm_pre_ffn), peer_w_query.astype(BF16), keys, B, S, _block(S, 512))

    tg = _block(S, 512)
    r1, e1, p2, e2 = _route(ss, tg)

    n_exp = peer_u.shape[0]
    u3 = peer_u.astype(BF16).reshape(n_exp // PEER_SUB, PEER_SUB, D)
    vt3 = peer_v.astype(BF16).reshape(n_exp // PEER_SUB, PEER_SUB, D).transpose(0, 2, 1)
    out = _peer(h2t, u3, vt3, r1, e1, p2, e2, x1, mod3, row(norm_post_ffn), B, S, tg,
                PEER_STEP_EXPERTS // PEER_SUB)
    return out.reshape(B, S, D)


def kernel(x, c, w_ada, b_ada, norm_pre_mix, norm_post_mix, w_in, w_alpha, b_alpha, gla_norm, conv_w, conv_b, lru_w_a, lru_b_a, lru_w_x, lru_b_x, lru_lambda, w_out, norm_pre_ffn, norm_post_ffn, peer_w_query, peer_sub_keys, peer_u, peer_v):
    depth = w_ada.shape[0]
    for l in range(depth):
        x = _layer(x, c, w_ada[l], b_ada[l], norm_pre_mix[l], norm_post_mix[l], w_in[l],
                   w_alpha[l], b_alpha[l], gla_norm[l], conv_w[l], conv_b[l], lru_w_a[l],
                   lru_b_a[l], lru_w_x[l], lru_b_x[l], lru_lambda[l], w_out[l],
                   norm_pre_ffn[l], norm_post_ffn[l], peer_w_query[l], peer_sub_keys[l],
                   peer_u[l], peer_v[l])
    return x
```
